```python
import math
import jax, jax.numpy as jnp
from jax import lax
import numpy as np

D_MODEL = 1024
BATCH = 8
SEQ = 4096
DEPTH = 1

CHUNK = 64
LEFT_CHUNKS = 8
BAND = (LEFT_CHUNKS + 1) * CHUNK
HEAD_DIM = 64
A_HEADS = 8
A_WIDTH = A_HEADS * HEAD_DIM
REL_CLIP = 256
B_HEADS = 4
B_QK_WIDTH = B_HEADS * 2 * HEAD_DIM
B_V_WIDTH = B_HEADS * 2 * HEAD_DIM
ROPE_THETA = 500000.0
ROT_DIM = HEAD_DIM // 4
D_FF = 2816
Q_BLOCK = 128
EPS = 1e-6
NEG = -1e30
IN_SIZES = (A_WIDTH, A_WIDTH, A_WIDTH, B_QK_WIDTH, B_QK_WIDTH, B_V_WIDTH, D_MODEL, D_MODEL)
IN_WIDTH = sum(IN_SIZES)
IN_SPLITS = np.cumsum(IN_SIZES)[:-1].tolist()

kernel_name = 'hybrid_chunk_diff_macaron'


def rms_norm(x, g):
    xf = x.astype(jnp.float32)
    y = xf * lax.rsqrt(jnp.mean(xf * xf, axis=-1, keepdims=True) + EPS)
    return (y * g.astype(jnp.float32)).astype(x.dtype)


def swiglu(x, w_gu, w_down):
    g, u = jnp.split(x @ w_gu, 2, axis=-1)
    return (jax.nn.silu(g) * u) @ w_down


def partial_rotary(x):
    s = x.shape[1]
    pos = jnp.arange(s, dtype=jnp.float32)
    inv = ROPE_THETA ** (-jnp.arange(0, ROT_DIM, 2, dtype=jnp.float32) / ROT_DIM)
    ang = pos[:, None] * inv[None, :]
    cos = jnp.concatenate([jnp.cos(ang)] * 2, axis=-1)[None, :, None, :]
    sin = jnp.concatenate([jnp.sin(ang)] * 2, axis=-1)[None, :, None, :]
    xr = x[..., :ROT_DIM].astype(jnp.float32)
    x1, x2 = xr[..., :ROT_DIM // 2], xr[..., ROT_DIM // 2:]
    rot = jnp.concatenate([-x2, x1], axis=-1)
    xr = (xr * cos + rot * sin).astype(x.dtype)
    return jnp.concatenate([xr, x[..., ROT_DIM:]], axis=-1)


def chunk_band_attention(q, k, v, rel_table):
    b, s, h, d = q.shape
    nc = s // CHUNK
    pad = LEFT_CHUNKS * CHUNK
    kp = jnp.pad(k, ((0, 0), (pad, 0), (0, 0), (0, 0)))
    vp = jnp.pad(v, ((0, 0), (pad, 0), (0, 0), (0, 0)))
    i = jnp.arange(CHUNK)[:, None]
    j = jnp.arange(BAND)[None, :]
    rel = i - j + pad
    idx = jnp.clip(rel, -REL_CLIP, REL_CLIP) + REL_CLIP
    bias = jnp.transpose(rel_table[idx], (2, 0, 1)).astype(jnp.float32)
    scale = 1.0 / math.sqrt(d)

    def one_chunk(c):
        qc = lax.dynamic_slice_in_dim(q, c * CHUNK, CHUNK, axis=1)
        kb = lax.dynamic_slice_in_dim(kp, c * CHUNK, BAND, axis=1)
        vb = lax.dynamic_slice_in_dim(vp, c * CHUNK, BAND, axis=1)
        sc = jnp.einsum('bqhd,bkhd->bhqk', qc, kb).astype(jnp.float32) * scale + bias
        valid = (j >= pad - c * CHUNK)[None, None]
        p = jax.nn.softmax(jnp.where(valid, sc, NEG), axis=-1).astype(v.dtype)
        return jnp.einsum('bhqk,bkhd->bqhd', p, vb)

    out = lax.map(one_chunk, jnp.arange(nc))
    return jnp.transpose(out, (1, 0, 2, 3, 4)).reshape(b, s, h * d)


def diff_attention(q1, q2, k1, k2, v, lam, g_sub, lambda_init):
    b, s, h, d = q1.shape
    nb = s // Q_BLOCK
    kchunk = jnp.arange(s) // CHUNK
    scale = 1.0 / math.sqrt(d)

    def one_block(blk):
        st = blk * Q_BLOCK
        q1b = lax.dynamic_slice_in_dim(q1, st, Q_BLOCK, axis=1)
        q2b = lax.dynamic_slice_in_dim(q2, st, Q_BLOCK, axis=1)
        qchunk = (st + jnp.arange(Q_BLOCK)) // CHUNK
        mask = (kchunk[None, :] <= qchunk[:, None])[None, None]
        s1 = jnp.einsum('bqhd,bkhd->bhqk', q1b, k1).astype(jnp.float32) * scale
        s2 = jnp.einsum('bqhd,bkhd->bhqk', q2b, k2).astype(jnp.float32) * scale
        p1 = jax.nn.softmax(jnp.where(mask, s1, NEG), axis=-1)
        p2 = jax.nn.softmax(jnp.where(mask, s2, NEG), axis=-1)
        a = (p1 - lam * p2).astype(v.dtype)
        o = jnp.einsum('bhqk,bkhe->bqhe', a, v)
        return rms_norm(o, g_sub) * (1.0 - lambda_init)

    out = lax.map(one_block, jnp.arange(nb))
    return jnp.transpose(out, (1, 0, 2, 3, 4)).reshape(b, s, h * 2 * d)


def setup_inputs(seed: int = 0) -> dict:
    key = jax.random.key(seed)
    ks = jax.random.split(key, 24)
    f = jnp.float32

    def w(k, shape, fan_in):
        return jax.random.normal(k, shape, f) * fan_in ** -0.5

    def gain(k, n):
        return 1.0 + 0.01 * jax.random.normal(k, (DEPTH, n), f)

    return {
        'x': jax.random.normal(ks[0], (BATCH, SEQ, D_MODEL), f),
        'g_ffn1': gain(ks[1], D_MODEL),
        'w_ffn1_gu': w(ks[2], (DEPTH, D_MODEL, 2 * D_FF), D_MODEL),
        'w_ffn1_down': w(ks[3], (DEPTH, D_FF, D_MODEL), D_FF),
        'g_mix': gain(ks[4], D_MODEL),
        'w_in': w(ks[5], (DEPTH, D_MODEL, IN_WIDTH), D_MODEL),
        'qn_a': gain(ks[6], HEAD_DIM),
        'kn_a': gain(ks[7], HEAD_DIM),
        'rel_bias': 0.1 * jax.random.normal(ks[8], (DEPTH, 2 * REL_CLIP + 1, A_HEADS), f),
        'qn_b': gain(ks[9], HEAD_DIM),
        'kn_b': gain(ks[10], HEAD_DIM),
        'lambda_q1': 0.1 * jax.random.normal(ks[11], (DEPTH, HEAD_DIM), f),
        'lambda_k1': 0.1 * jax.random.normal(ks[12], (DEPTH, HEAD_DIM), f),
        'lambda_q2': 0.1 * jax.random.normal(ks[13], (DEPTH, HEAD_DIM), f),
        'lambda_k2': 0.1 * jax.random.normal(ks[14], (DEPTH, HEAD_DIM), f),
        'g_subln': gain(ks[15], 2 * HEAD_DIM),
        'w_up_a': w(ks[16], (DEPTH, A_WIDTH, D_MODEL), A_WIDTH),
        'w_up_b': w(ks[17], (DEPTH, B_V_WIDTH, D_MODEL), B_V_WIDTH),
        'w_out': w(ks[18], (DEPTH, D_MODEL, D_MODEL), D_MODEL),
        'g_ffn2': gain(ks[19], D_MODEL),
        'w_ffn2_gu': w(ks[20], (DEPTH, D_MODEL, 2 * D_FF), D_MODEL),
        'w_ffn2_down': w(ks[21], (DEPTH, D_FF, D_MODEL), D_FF),
        'g_final': gain(ks[22], D_MODEL),
    }


def reference(x, g_ffn1, w_ffn1_gu, w_ffn1_down, g_mix, w_in, qn_a, kn_a, rel_bias,
              qn_b, kn_b, lambda_q1, lambda_k1, lambda_q2, lambda_k2, g_subln,
              w_up_a, w_up_b, w_out, g_ffn2, w_ffn2_gu, w_ffn2_down, g_final):
    b, s, _ = x.shape
    for l in range(DEPTH):
        x = x + 0.5 * swiglu(rms_norm(x, g_ffn1[l]), w_ffn1_gu[l], w_ffn1_down[l])

        h = rms_norm(x, g_mix[l])
        qa, ka, va, qb, kb, vb, ga, gb = jnp.split(h @ w_in[l], IN_SPLITS, axis=-1)

        qa = rms_norm(qa.reshape(b, s, A_HEADS, HEAD_DIM), qn_a[l])
        ka = rms_norm(ka.reshape(b, s, A_HEADS, HEAD_DIM), kn_a[l])
        va = va.reshape(b, s, A_HEADS, HEAD_DIM)
        oa = chunk_band_attention(qa, ka, va, rel_bias[l])

        qb = partial_rotary(rms_norm(qb.reshape(b, s, 2 * B_HEADS, HEAD_DIM), qn_b[l]))
        kb = partial_rotary(rms_norm(kb.reshape(b, s, 2 * B_HEADS, HEAD_DIM), kn_b[l]))
        vb = vb.reshape(b, s, B_HEADS, 2 * HEAD_DIM)
        lambda_init = 0.8 - 0.6 * math.exp(-0.3 * l)
        lam = (jnp.exp(jnp.sum(lambda_q1[l].astype(jnp.float32) * lambda_k1[l].astype(jnp.float32)))
               - jnp.exp(jnp.sum(lambda_q2[l].astype(jnp.float32) * lambda_k2[l].astype(jnp.float32)))
               + lambda_init)
        ob = diff_attention(qb[:, :, 0::2], qb[:, :, 1::2], kb[:, :, 0::2], kb[:, :, 1::2],
                            vb, lam, g_subln[l], lambda_init)

        y = jax.nn.sigmoid(ga) * (oa @ w_up_a[l]) + jax.nn.sigmoid(gb) * (ob @ w_up_b[l])
        x = x + y @ w_out[l]

        x = x + 0.5 * swiglu(rms_norm(x, g_ffn2[l]), w_ffn2_gu[l], w_ffn2_down[l])
        x = rms_norm(x, g_final[l])
    return x
```

```python
import functools
import math

import jax
import jax.numpy as jnp
from jax import lax
from jax.experimental import pallas as pl
from jax.experimental.pallas import tpu as pltpu

F32 = jnp.float32
BF16 = jnp.bfloat16

D_MODEL = 1024
D_FF = 2816
CHUNK = 64
LEFT_CHUNKS = 8
HEAD_DIM = 64
A_HEADS = 8
A_WIDTH = A_HEADS * HEAD_DIM
REL_CLIP = 256
B_HEADS = 4
B_WIDTH = B_HEADS * 2 * HEAD_DIM
ROPE_THETA = 500000.0
ROT_DIM = HEAD_DIM // 4
EPS = 1e-6
NEG = -1e30
IN_WIDTH = 3 * A_WIDTH + 3 * B_WIDTH + 2 * D_MODEL
LAMBDA_INIT = 0.8 - 0.6 * math.exp(-0.3 * 0)

LANES = 128
TOKEN_TILE = 512
FF_TILE = 256
A_Q_TILE = 128
A_BAND = A_Q_TILE + LEFT_CHUNKS * CHUNK
A_BIAS_W = 2 * LEFT_CHUNKS * CHUNK + A_BAND
B_Q_TILE = 256
B_K_TILE = 256
VMEM_LIMIT = 56 * 1024 * 1024


def _resident(shape):
    return pl.BlockSpec(shape, lambda *_: (0,) * len(shape), pipeline_mode=pl.Buffered(1))


def _rms(x, g):
    ms = jnp.mean(x * x, axis=-1, keepdims=True)
    return x * lax.rsqrt(ms + EPS) * g


def _swiglu(xn, wgu_ref, wdn_ref, acc_ref):
    acc_ref[...] = jnp.zeros_like(acc_ref)

    def step(c, carry):
        lo = pl.multiple_of(c * FF_TILE, FF_TILE)
        g = jnp.dot(xn, wgu_ref[:, pl.ds(lo, FF_TILE)], preferred_element_type=F32)
        u = jnp.dot(xn, wgu_ref[:, pl.ds(D_FF + lo, FF_TILE)], preferred_element_type=F32)
        a = (g * jax.nn.sigmoid(g) * u).astype(BF16)
        acc_ref[...] += jnp.dot(a, wdn_ref[pl.ds(lo, FF_TILE), :], preferred_element_type=F32)
        return carry

    lax.fori_loop(0, D_FF // FF_TILE, step, 0)


def _ffn1_kernel(x_ref, g_ref, wgu_ref, wdn_ref, o_ref, acc_ref):
    x = x_ref[...]
    _swiglu(_rms(x, g_ref[...]).astype(BF16), wgu_ref, wdn_ref, acc_ref)
    o_ref[...] = x + 0.5 * acc_ref[...]


def _ffn1(x, g, wgu, wdn):
    t = x.shape[0]
    row = pl.BlockSpec((TOKEN_TILE, D_MODEL), lambda i: (i, 0))
    return pl.pallas_call(
        _ffn1_kernel,
        grid=(t // TOKEN_TILE,),
        in_specs=[row, _resident((1, D_MODEL)), _resident(wgu.shape), _resident(wdn.shape)],
        out_specs=row,
        out_shape=jax.ShapeDtypeStruct(x.shape, F32),
        scratch_shapes=[pltpu.VMEM((TOKEN_TILE, D_MODEL), F32)],
        compiler_params=pltpu.CompilerParams(
            dimension_semantics=("arbitrary",), vmem_limit_bytes=VMEM_LIMIT),
        name="ffn1",
    )(x, g, wgu, wdn)


def _head_norm(x, g2):
    lane = lax.broadcasted_iota(jnp.int32, x.shape, 1)
    lo = lane < HEAD_DIM
    sq = x * x
    s_lo = jnp.sum(jnp.where(lo, sq, 0.0), axis=-1, keepdims=True)
    s_hi = jnp.sum(jnp.where(lo, 0.0, sq), axis=-1, keepdims=True)
    ms = jnp.where(lo, s_lo, s_hi) * (1.0 / HEAD_DIM)
    return x * lax.rsqrt(ms + EPS) * g2


def _rotary(x, cos, sin_lo, sin_hi):
    half = ROT_DIM // 2
    up = pltpu.roll(x, LANES - half, 1)
    dn = pltpu.roll(x, half, 1)
    return x * cos + up * sin_lo + dn * sin_hi


def _in_proj_kernel(x_ref, g_ref, w_ref, qna_ref, kna_ref, qnb_ref, knb_ref,
                    cos_ref, sl_ref, sh_ref,
                    qa_ref, ka_ref, va_ref, qb_ref, kb_ref, vb_ref, gate_ref):
    h = _rms(x_ref[...], g_ref[...]).astype(BF16)
    scale = 1.0 / math.sqrt(HEAD_DIM)

    def proj(col, width):
        return jnp.dot(h, w_ref[:, col:col + width], preferred_element_type=F32)

    def normed(col, gain_ref, out_ref, rotate, mult):
        y = proj(col, A_WIDTH)
        for p in range(A_WIDTH // LANES):
            t = _head_norm(y[:, p * LANES:(p + 1) * LANES], gain_ref[...])
            if rotate:
                t = _rotary(t, cos_ref[...], sl_ref[...], sh_ref[...])
            if mult != 1.0:
                t = t * mult
            out_ref[:, p * LANES:(p + 1) * LANES] = t.astype(BF16)

    normed(0 * A_WIDTH, qna_ref, qa_ref, False, scale)
    normed(1 * A_WIDTH, kna_ref, ka_ref, False, 1.0)
    va_ref[...] = proj(2 * A_WIDTH, A_WIDTH).astype(BF16)
    normed(3 * A_WIDTH, qnb_ref, qb_ref, True, scale)
    normed(4 * A_WIDTH, knb_ref, kb_ref, True, 1.0)
    vb_ref[...] = proj(5 * A_WIDTH, B_WIDTH).astype(BF16)
    for c in range(2 * D_MODEL // A_WIDTH):
        gate_ref[:, c * A_WIDTH:(c + 1) * A_WIDTH] = proj(6 * A_WIDTH + c * A_WIDTH, A_WIDTH)


def _in_proj(x1, g, w_in, qna, kna, qnb, knb, cos, sin_lo, sin_hi, seq):
    t = x1.shape[0]
    tiles_per_seq = seq // TOKEN_TILE
    row = lambda w: pl.BlockSpec((TOKEN_TILE, w), lambda i: (i, 0))
    pos = pl.BlockSpec((TOKEN_TILE, LANES), lambda i: (i % tiles_per_seq, 0))
    gain = _resident((1, LANES))
    half = jax.ShapeDtypeStruct((t, A_WIDTH), BF16)
    return pl.pallas_call(
        _in_proj_kernel,
        grid=(t // TOKEN_TILE,),
        in_specs=[row(D_MODEL), _resident((1, D_MODEL)), _resident(w_in.shape),
                  gain, gain, gain, gain, pos, pos, pos],
        out_specs=[row(A_WIDTH)] * 6 + [row(2 * D_MODEL)],
        out_shape=[half] * 6 + [jax.ShapeDtypeStruct((t, 2 * D_MODEL), F32)],
        compiler_params=pltpu.CompilerParams(
            dimension_semantics=("arbitrary",), vmem_limit_bytes=VMEM_LIMIT),
        name="in_proj",
    )(x1, g, w_in, qna, kna, qnb, knb, cos, sin_lo, sin_hi)


def _attn_a_kernel(q_ref, k_ref, v_ref, bias_ref, o_ref):
    i = pl.program_id(1)
    pad = LEFT_CHUNKS * CHUNK
    start = pl.multiple_of(jnp.maximum(i * A_Q_TILE - pad, 0), A_Q_TILE)
    off = pl.multiple_of(2 * pad - i * A_Q_TILE + start, A_Q_TILE)
    lane = lax.broadcasted_iota(jnp.int32, (A_Q_TILE, LANES), 1)
    lo = lane < HEAD_DIM
    for p in range(A_WIDTH // LANES):
        cols = slice(p * LANES, (p + 1) * LANES)
        q = q_ref[0, :, cols]
        k = k_ref[0, pl.ds(start, A_BAND), cols]
        v = v_ref[0, pl.ds(start, A_BAND), cols]
        outs = []
        for e in range(2):
            qm = jnp.where(lo if e == 0 else jnp.logical_not(lo), q, jnp.zeros_like(q))
            s = lax.dot_general(qm, k, (((1,), (1,)), ((), ())), preferred_element_type=F32)
            s = s + bias_ref[2 * p + e, :, pl.ds(off, A_BAND)]
            m = jnp.max(s, axis=-1, keepdims=True)
            pr = jnp.exp(s - m)
            l = jnp.sum(pr, axis=-1, keepdims=True)
            o = jnp.dot(pr.astype(BF16), v, preferred_element_type=F32)
            outs.append(o / l)
        o_ref[0, :, cols] = jnp.where(lo, outs[0], outs[1]).astype(BF16)


def _attn_a(q, k, v, bias):
    b, s, _ = q.shape
    whole = pl.BlockSpec((1, s, A_WIDTH), lambda bi, i: (bi, 0, 0))
    tile = pl.BlockSpec((1, A_Q_TILE, A_WIDTH), lambda bi, i: (bi, i, 0))
    return pl.pallas_call(
        _attn_a_kernel,
        grid=(b, s // A_Q_TILE),
        in_specs=[tile, whole, whole, _resident(bias.shape)],
        out_specs=tile,
        out_shape=jax.ShapeDtypeStruct(q.shape, BF16),
        compiler_params=pltpu.CompilerParams(
            dimension_semantics=("arbitrary", "arbitrary"), vmem_limit_bytes=VMEM_LIMIT),
        name="attn_a",
    )(q, k, v, bias)


def _band_bias(rel_table):
    pad = LEFT_CHUNKS * CHUNK
    r = jnp.arange(A_Q_TILE)[:, None]
    c = jnp.arange(A_BIAS_W)[None, :]
    rel = r - c + 2 * pad
    q_chunk = r // CHUNK
    k_chunk = jnp.floor_divide(r - rel, CHUNK)
    valid = (k_chunk <= q_chunk) & (k_chunk >= q_chunk - LEFT_CHUNKS)
    idx = jnp.clip(rel, -REL_CLIP, REL_CLIP) + REL_CLIP
    bias = jnp.transpose(rel_table[idx], (2, 0, 1)).astype(F32)
    return jnp.where(valid[None], bias, NEG)


def _attn_b_kernel(q_ref, k_ref, v_ref, lam_ref, gs_ref, o_ref, m_ref, l_ref, acc_ref):
    i = pl.program_id(2)
    q = q_ref[0]
    lane = lax.broadcasted_iota(jnp.int32, q.shape, 1)
    zero = jnp.zeros_like(q)
    qs = jnp.concatenate([jnp.where(lane < HEAD_DIM, q, zero),
                          jnp.where(lane < HEAD_DIM, zero, q)], axis=0)
    m_ref[...] = jnp.full_like(m_ref, NEG)
    l_ref[...] = jnp.zeros_like(l_ref)
    acc_ref[...] = jnp.zeros_like(acc_ref)

    def block(j, masked):
        lo = pl.multiple_of(j * B_K_TILE, B_K_TILE)
        k = k_ref[0, pl.ds(lo, B_K_TILE), :]
        v = v_ref[0, pl.ds(lo, B_K_TILE), :]
        s = lax.dot_general(qs, k, (((1,), (1,)), ((), ())), preferred_element_type=F32)
        if masked:
            row = lax.broadcasted_iota(jnp.int32, s.shape, 0) % B_Q_TILE
            col = lax.broadcasted_iota(jnp.int32, s.shape, 1)
            s = jnp.where(col // CHUNK <= row // CHUNK, s, NEG)
        m_old = m_ref[...]
        m_new = jnp.maximum(m_old, jnp.max(s, axis=-1, keepdims=True))
        alpha = jnp.exp(m_old - m_new)
        p = jnp.exp(s - m_new)
        l_ref[...] = alpha * l_ref[...] + jnp.sum(p, axis=-1, keepdims=True)
        acc_ref[...] = alpha * acc_ref[...] + jnp.dot(p.astype(BF16), v, preferred_element_type=F32)
        m_ref[...] = m_new

    def body(j, carry):
        block(j, False)
        return carry

    lax.fori_loop(0, i, body, 0)
    block(i, True)

    lq1, lk1, lq2, lk2 = (lam_ref[n:n + 1, :] for n in range(4))
    lam = (jnp.exp(jnp.sum(lq1 * lk1, axis=-1, keepdims=True))
           - jnp.exp(jnp.sum(lq2 * lk2, axis=-1, keepdims=True)) + LAMBDA_INIT)
    o = acc_ref[...] / l_ref[...]
    o = o[:B_Q_TILE] - lam * o[B_Q_TILE:]
    o_ref[0] = (_rms(o, gs_ref[...]) * (1.0 - LAMBDA_INIT)).astype(BF16)


def _attn_b(q, k, v, lam_vecs, g_sub):
    b, s, _ = q.shape
    whole = pl.BlockSpec((1, s, LANES), lambda bi, h, i: (bi, 0, h))
    tile = pl.BlockSpec((1, B_Q_TILE, LANES), lambda bi, h, i: (bi, i, h))
    return pl.pallas_call(
        _attn_b_kernel,
        grid=(b, B_HEADS, s // B_Q_TILE),
        in_specs=[tile, whole, whole, _resident(lam_vecs.shape), _resident(g_sub.shape)],
        out_specs=tile,
        out_shape=jax.ShapeDtypeStruct(q.shape, BF16),
        scratch_shapes=[pltpu.VMEM((2 * B_Q_TILE, 1), F32), pltpu.VMEM((2 * B_Q_TILE, 1), F32),
                        pltpu.VMEM((2 * B_Q_TILE, LANES), F32)],
        compiler_params=pltpu.CompilerParams(
            dimension_semantics=("arbitrary", "arbitrary", "arbitrary"),
            vmem_limit_bytes=VMEM_LIMIT),
        name="attn_b",
    )(q, k, v, lam_vecs, g_sub)


def _post_kernel(x_ref, oa_ref, ob_ref, gate_ref, wa_ref, wb_ref, wo_ref,
                 g2_ref, wgu_ref, wdn_ref, gf_ref, o_ref, acc_ref):
    ya = jnp.dot(oa_ref[...], wa_ref[...], preferred_element_type=F32)
    yb = jnp.dot(ob_ref[...], wb_ref[...], preferred_element_type=F32)
    y = (jax.nn.sigmoid(gate_ref[:, :D_MODEL]) * ya
         + jax.nn.sigmoid(gate_ref[:, D_MODEL:]) * yb)
    x2 = x_ref[...] + jnp.dot(y.astype(BF16), wo_ref[...], preferred_element_type=F32)
    _swiglu(_rms(x2, g2_ref[...]).astype(BF16), wgu_ref, wdn_ref, acc_ref)
    o_ref[...] = _rms(x2 + 0.5 * acc_ref[...], gf_ref[...])


def _post(x1, oa, ob, gates, wa, wb, wo, g2, wgu, wdn, gf):
    t = x1.shape[0]
    row = lambda w: pl.BlockSpec((TOKEN_TILE, w), lambda i: (i, 0))
    vec = _resident((1, D_MODEL))
    return pl.pallas_call(
        _post_kernel,
        grid=(t // TOKEN_TILE,),
        in_specs=[row(D_MODEL), row(A_WIDTH), row(B_WIDTH), row(2 * D_MODEL),
                  _resident(wa.shape), _resident(wb.shape), _resident(wo.shape),
                  vec, _resident(wgu.shape), _resident(wdn.shape), vec],
        out_specs=row(D_MODEL),
        out_shape=jax.ShapeDtypeStruct(x1.shape, F32),
        scratch_shapes=[pltpu.VMEM((TOKEN_TILE, D_MODEL), F32)],
        compiler_params=pltpu.CompilerParams(
            dimension_semantics=("arbitrary",), vmem_limit_bytes=VMEM_LIMIT),
        name="post",
    )(x1, oa, ob, gates, wa, wb, wo, g2, wgu, wdn, gf)


def _rotary_tables(seq):
    half = ROT_DIM // 2
    pos = jnp.arange(seq, dtype=F32)
    inv = ROPE_THETA ** (-jnp.arange(0, ROT_DIM, 2, dtype=F32) / ROT_DIM)
    ang = pos[:, None] * inv[None, :]
    cos, sin = jnp.cos(ang), jnp.sin(ang)
    ones = jnp.ones((seq, HEAD_DIM - ROT_DIM), F32)
    zeros = jnp.zeros((seq, HEAD_DIM - half), F32)
    cos_h = jnp.concatenate([cos, cos, ones], axis=-1)
    sin_lo = jnp.concatenate([-sin, zeros], axis=-1)
    sin_hi = jnp.concatenate([jnp.zeros((seq, half), F32), sin,
                              jnp.zeros((seq, HEAD_DIM - ROT_DIM), F32)], axis=-1)
    two = lambda a: jnp.concatenate([a, a], axis=-1)
    return two(cos_h), two(sin_lo), two(sin_hi)


def kernel(x, g_ffn1, w_ffn1_gu, w_ffn1_down, g_mix, w_in, qn_a, kn_a, rel_bias, qn_b, kn_b,
           lambda_q1, lambda_k1, lambda_q2, lambda_k2, g_subln, w_up_a, w_up_b, w_out, g_ffn2,
           w_ffn2_gu, w_ffn2_down, g_final):
    b, s, d = x.shape
    assert d == D_MODEL and s % TOKEN_TILE == 0 and g_ffn1.shape[0] == 1
    l = 0
    t = b * s
    two = lambda g: jnp.concatenate([g, g], axis=-1)[None, :]
    cos, sin_lo, sin_hi = _rotary_tables(s)

    x1 = _ffn1(x.reshape(t, d), g_ffn1[l][None], w_ffn1_gu[l].astype(BF16),
               w_ffn1_down[l].astype(BF16))
    qa, ka, va, qb, kb, vb, gates = _in_proj(
        x1, g_mix[l][None], w_in[l].astype(BF16), two(qn_a[l]), two(kn_a[l]), two(qn_b[l]),
        two(kn_b[l]), cos, sin_lo, sin_hi, s)
    seq3 = lambda a: a.reshape(b, s, a.shape[-1])
    oa = _attn_a(seq3(qa), seq3(ka), seq3(va), _band_bias(rel_bias[l]))
    lam_vecs = jnp.stack([lambda_q1[l], lambda_k1[l], lambda_q2[l], lambda_k2[l]]).astype(F32)
    ob = _attn_b(seq3(qb), seq3(kb), seq3(vb), lam_vecs, g_subln[l][None])
    out = _post(x1, oa.reshape(t, A_WIDTH), ob.reshape(t, B_WIDTH), gates,
                w_up_a[l].astype(BF16), w_up_b[l].astype(BF16), w_out[l].astype(BF16),
                g_ffn2[l][None], w_ffn2_gu[l].astype(BF16), w_ffn2_down[l].astype(BF16),
                g_final[l][None])
    return out.reshape(b, s, d)
```

```python
import functools
import math

import jax
import jax.numpy as jnp
from jax import lax
from jax.experimental import pallas as pl
from jax.experimental.pallas import tpu as pltpu

F32 = jnp.float32
BF16 = jnp.bfloat16

D_MODEL = 1024
D_FF = 2816
CHUNK = 64
LEFT_CHUNKS = 8
HEAD_DIM = 64
A_HEADS = 8
A_WIDTH = A_HEADS * HEAD_DIM
REL_CLIP = 256
B_HEADS = 4
B_WIDTH = B_HEADS * 2 * HEAD_DIM
ROPE_THETA = 500000.0
ROT_DIM = HEAD_DIM // 4
EPS = 1e-6
NEG = -1e30
IN_WIDTH = 3 * A_WIDTH + 3 * B_WIDTH + 2 * D_MODEL
LAMBDA_INIT = 0.8 - 0.6 * math.exp(-0.3 * 0)

LANES = 128
TOKEN_TILE = 512
FF_TILE = 256
A_Q_TILE = 128
A_BAND = A_Q_TILE + LEFT_CHUNKS * CHUNK
A_BIAS_W = 2 * LEFT_CHUNKS * CHUNK + A_BAND
A_TABLE_W = 2048
B_Q_TILE = 256
B_K_TILE = 256
B_SHIFT_SAFE = 40.0
VMEM_LIMIT = 56 * 1024 * 1024


def _resident(shape):
    return pl.BlockSpec(shape, lambda *_: (0,) * len(shape), pipeline_mode=pl.Buffered(1))


def _rms(x, g):
    ms = jnp.mean(x * x, axis=-1, keepdims=True)
    return x * lax.rsqrt(ms + EPS) * g


def _swiglu(xn, wgu_ref, wdn_ref, acc_ref):
    acc_ref[...] = jnp.zeros_like(acc_ref)

    def step(c, carry):
        lo = pl.multiple_of(c * FF_TILE, FF_TILE)
        g = jnp.dot(xn, wgu_ref[:, pl.ds(lo, FF_TILE)], preferred_element_type=F32)
        u = jnp.dot(xn, wgu_ref[:, pl.ds(D_FF + lo, FF_TILE)], preferred_element_type=F32)
        a = (g * jax.nn.sigmoid(g) * u).astype(BF16)
        acc_ref[...] += jnp.dot(a, wdn_ref[pl.ds(lo, FF_TILE), :], preferred_element_type=F32)
        return carry

    lax.fori_loop(0, D_FF // FF_TILE, step, 0)


def _ffn1_kernel(x_ref, g_ref, wgu_ref, wdn_ref, o_ref, acc_ref):
    x = x_ref[...]
    _swiglu(_rms(x, g_ref[...]).astype(BF16), wgu_ref, wdn_ref, acc_ref)
    o_ref[...] = x + 0.5 * acc_ref[...]


def _ffn1(x, g, wgu, wdn):
    t = x.shape[0]
    row = pl.BlockSpec((TOKEN_TILE, D_MODEL), lambda i: (i, 0))
    return pl.pallas_call(
        _ffn1_kernel,
        grid=(t // TOKEN_TILE,),
        in_specs=[row, _resident((1, D_MODEL)), _resident(wgu.shape), _resident(wdn.shape)],
        out_specs=row,
        out_shape=jax.ShapeDtypeStruct(x.shape, F32),
        scratch_shapes=[pltpu.VMEM((TOKEN_TILE, D_MODEL), F32)],
        compiler_params=pltpu.CompilerParams(
            dimension_semantics=("arbitrary",), vmem_limit_bytes=VMEM_LIMIT),
        name="ffn1",
    )(x, g, wgu, wdn)


def _head_norm(x, g2):
    lane = lax.broadcasted_iota(jnp.int32, x.shape, 1)
    lo = lane < HEAD_DIM
    sq = x * x
    s_lo = jnp.sum(jnp.where(lo, sq, 0.0), axis=-1, keepdims=True)
    s_hi = jnp.sum(jnp.where(lo, 0.0, sq), axis=-1, keepdims=True)
    ms = jnp.where(lo, s_lo, s_hi) * (1.0 / HEAD_DIM)
    return x * lax.rsqrt(ms + EPS) * g2


def _rotary(x, cos, sin_lo, sin_hi):
    half = ROT_DIM // 2
    up = pltpu.roll(x, LANES - half, 1)
    dn = pltpu.roll(x, half, 1)
    return x * cos + up * sin_lo + dn * sin_hi


def _in_proj_kernel(x_ref, g_ref, w_ref, qna_ref, kna_ref, qnb_ref, knb_ref,
                    cos_ref, sl_ref, sh_ref,
                    qa_ref, ka_ref, va_ref, qb_ref, kb_ref, vb_ref, gate_ref):
    h = _rms(x_ref[...], g_ref[...]).astype(BF16)
    scale = 1.0 / math.sqrt(HEAD_DIM)

    def proj(col, width):
        return jnp.dot(h, w_ref[:, col:col + width], preferred_element_type=F32)

    def normed(col, gain_ref, out_ref, rotate, mult):
        y = proj(col, A_WIDTH)
        for p in range(A_WIDTH // LANES):
            t = _head_norm(y[:, p * LANES:(p + 1) * LANES], gain_ref[...])
            if rotate:
                t = _rotary(t, cos_ref[...], sl_ref[...], sh_ref[...])
            if mult != 1.0:
                t = t * mult
            out_ref[:, p * LANES:(p + 1) * LANES] = t.astype(BF16)

    normed(0 * A_WIDTH, qna_ref, qa_ref, False, scale)
    normed(1 * A_WIDTH, kna_ref, ka_ref, False, 1.0)
    va_ref[...] = proj(2 * A_WIDTH, A_WIDTH).astype(BF16)
    normed(3 * A_WIDTH, qnb_ref, qb_ref, True, scale)
    normed(4 * A_WIDTH, knb_ref, kb_ref, True, 1.0)
    vb_ref[...] = proj(5 * A_WIDTH, B_WIDTH).astype(BF16)
    for c in range(2 * D_MODEL // A_WIDTH):
        gate_ref[:, c * A_WIDTH:(c + 1) * A_WIDTH] = proj(6 * A_WIDTH + c * A_WIDTH, A_WIDTH)


def _in_proj(x1, g, w_in, qna, kna, qnb, knb, cos, sin_lo, sin_hi, seq):
    t = x1.shape[0]
    tiles_per_seq = seq // TOKEN_TILE
    row = lambda w: pl.BlockSpec((TOKEN_TILE, w), lambda i: (i, 0))
    pos = pl.BlockSpec((TOKEN_TILE, LANES), lambda i: (i % tiles_per_seq, 0))
    gain = _resident((1, LANES))
    half = jax.ShapeDtypeStruct((t, A_WIDTH), BF16)
    return pl.pallas_call(
        _in_proj_kernel,
        grid=(t // TOKEN_TILE,),
        in_specs=[row(D_MODEL), _resident((1, D_MODEL)), _resident(w_in.shape),
                  gain, gain, gain, gain, pos, pos, pos],
        out_specs=[row(A_WIDTH)] * 6 + [row(2 * D_MODEL)],
        out_shape=[half] * 6 + [jax.ShapeDtypeStruct((t, 2 * D_MODEL), F32)],
        compiler_params=pltpu.CompilerParams(
            dimension_semantics=("arbitrary",), vmem_limit_bytes=VMEM_LIMIT),
        name="in_proj",
    )(x1, g, w_in, qna, kna, qnb, knb, cos, sin_lo, sin_hi)


def _fill_band_bias(f_ref, bias_ref):
    pad = LEFT_CHUNKS * CHUNK
    shape = (A_Q_TILE, A_BIAS_W)
    q_chunk = lax.broadcasted_iota(jnp.int32, shape, 0) >> 6
    k_chunk = (lax.broadcasted_iota(jnp.int32, shape, 1) - 2 * pad) >> 6
    valid = jnp.logical_and(k_chunk <= q_chunk, k_chunk >= q_chunk - LEFT_CHUNKS)
    for h in range(A_HEADS):
        row = jnp.broadcast_to(f_ref[h:h + 1, :], (A_Q_TILE, A_TABLE_W))
        skew = pltpu.roll(row, 0, 1, stride=1, stride_axis=0)
        bias_ref[h] = jnp.where(valid, skew[:, :A_BIAS_W], NEG)


def _attn_a_kernel(q_ref, k_ref, v_ref, f_ref, o_ref, bias_ref):
    i = pl.program_id(1)
    pad = LEFT_CHUNKS * CHUNK

    @pl.when(jnp.logical_and(pl.program_id(0) == 0, i == 0))
    def _():
        _fill_band_bias(f_ref, bias_ref)

    start = pl.multiple_of(jnp.maximum(i * A_Q_TILE - pad, 0), A_Q_TILE)
    off = pl.multiple_of(2 * pad - i * A_Q_TILE + start, A_Q_TILE)
    lane = lax.broadcasted_iota(jnp.int32, (A_Q_TILE, LANES), 1)
    lo = lane < HEAD_DIM
    for p in range(A_WIDTH // LANES):
        cols = slice(p * LANES, (p + 1) * LANES)
        q = q_ref[0, :, cols]
        k = k_ref[0, pl.ds(start, A_BAND), cols]
        v = v_ref[0, pl.ds(start, A_BAND), cols]
        outs = []
        for e in range(2):
            qm = jnp.where(lo if e == 0 else jnp.logical_not(lo), q, jnp.zeros_like(q))
            s = lax.dot_general(qm, k, (((1,), (1,)), ((), ())), preferred_element_type=F32)
            s = s + bias_ref[2 * p + e, :, pl.ds(off, A_BAND)]
            m = jnp.max(s, axis=-1, keepdims=True)
            pr = jnp.exp(s - m)
            l = jnp.sum(pr, axis=-1, keepdims=True)
            o = jnp.dot(pr.astype(BF16), v, preferred_element_type=F32)
            outs.append(o / l)
        o_ref[0, :, cols] = jnp.where(lo, outs[0], outs[1]).astype(BF16)


def _attn_a(q, k, v, f_table):
    b, s, _ = q.shape
    whole = pl.BlockSpec((1, s, A_WIDTH), lambda bi, i: (bi, 0, 0))
    tile = pl.BlockSpec((1, A_Q_TILE, A_WIDTH), lambda bi, i: (bi, i, 0))
    return pl.pallas_call(
        _attn_a_kernel,
        grid=(b, s // A_Q_TILE),
        in_specs=[tile, whole, whole, _resident(f_table.shape)],
        out_specs=tile,
        out_shape=jax.ShapeDtypeStruct(q.shape, BF16),
        scratch_shapes=[pltpu.VMEM((A_HEADS, A_Q_TILE, A_BIAS_W), F32)],
        compiler_params=pltpu.CompilerParams(
            dimension_semantics=("arbitrary", "arbitrary"), vmem_limit_bytes=VMEM_LIMIT),
        name="attn_a",
    )(q, k, v, f_table)


def _clipped_table(rel_table):
    pad = LEFT_CHUNKS * CHUNK
    first, last = rel_table[:1], rel_table[-1:]
    lead = 2 * pad - REL_CLIP
    tail = A_Q_TILE - 1
    mid = A_TABLE_W - lead - rel_table.shape[0] - tail
    f = jnp.concatenate([jnp.repeat(last, lead, axis=0), rel_table[::-1],
                         jnp.repeat(first, mid, axis=0), jnp.repeat(last, tail, axis=0)], axis=0)
    return f.T.astype(F32)


def _attn_b_kernel(bound_ref, q_ref, k_ref, v_ref, lam_ref, gs_ref, o_ref,
                   qs_ref, m_ref, l_ref, acc_ref):
    i = pl.program_id(1)
    nn = (((1,), (1,)), ((), ()))
    lane = lax.broadcasted_iota(jnp.int32, (B_Q_TILE, LANES), 1)
    first = lane < HEAD_DIM
    for h in range(B_HEADS):
        q = q_ref[0, :, h * LANES:(h + 1) * LANES]
        zero = jnp.zeros_like(q)
        qs_ref[h, :B_Q_TILE] = jnp.where(first, q, zero)
        qs_ref[h, B_Q_TILE:] = jnp.where(first, zero, q)
    l_ref[...] = jnp.zeros_like(l_ref)
    acc_ref[...] = jnp.zeros_like(acc_ref)

    def scores(h, lo, masked):
        k = k_ref[0, pl.ds(lo, B_K_TILE), h * LANES:(h + 1) * LANES]
        s = lax.dot_general(qs_ref[h], k, nn, preferred_element_type=F32)
        if masked:
            row = lax.broadcasted_iota(jnp.int32, s.shape, 0) & (B_Q_TILE - 1)
            col = lax.broadcasted_iota(jnp.int32, s.shape, 1)
            s = jnp.where((col >> 6) <= (row >> 6), s, NEG)
        return [s[:, c * LANES:(c + 1) * LANES] for c in range(B_K_TILE // LANES)]

    def max_block(j, masked):
        lo = pl.multiple_of(j * B_K_TILE, B_K_TILE)
        for h in range(B_HEADS):
            m_ref[h] = functools.reduce(jnp.maximum, scores(h, lo, masked), m_ref[h])

    def sum_block(j, masked):
        lo = pl.multiple_of(j * B_K_TILE, B_K_TILE)
        for h in range(B_HEADS):
            m = m_ref[h]
            ps = [jnp.exp(s - m) for s in scores(h, lo, masked)]
            l_ref[h] += functools.reduce(jnp.add, ps)
            v = v_ref[0, pl.ds(lo, B_K_TILE), h * LANES:(h + 1) * LANES]
            acc_ref[h] += jnp.dot(jnp.concatenate(ps, axis=1).astype(BF16), v,
                                  preferred_element_type=F32)

    def sweep(block):
        def body(j, carry):
            block(j, False)
            return carry
        lax.fori_loop(0, i, body, 0)
        block(i, True)

    bound = bound_ref[0]
    shift_by_bound = bound <= B_SHIFT_SAFE

    @pl.when(shift_by_bound)
    def _():
        m_ref[...] = jnp.full(m_ref.shape, bound, F32)

    @pl.when(jnp.logical_not(shift_by_bound))
    def _():
        m_ref[...] = jnp.full(m_ref.shape, NEG, F32)
        sweep(max_block)
        for h in range(B_HEADS):
            m_ref[h] = jnp.broadcast_to(jnp.max(m_ref[h], axis=-1, keepdims=True), m_ref.shape[1:])

    sweep(sum_block)

    lq1, lk1, lq2, lk2 = (lam_ref[n:n + 1, :] for n in range(4))
    lam = (jnp.exp(jnp.sum(lq1 * lk1, axis=-1, keepdims=True))
           - jnp.exp(jnp.sum(lq2 * lk2, axis=-1, keepdims=True)) + LAMBDA_INIT)
    for h in range(B_HEADS):
        o = acc_ref[h] / jnp.sum(l_ref[h], axis=-1, keepdims=True)
        o = o[:B_Q_TILE] - lam * o[B_Q_TILE:]
        o_ref[0, :, h * LANES:(h + 1) * LANES] = (
            _rms(o, gs_ref[...]) * (1.0 - LAMBDA_INIT)).astype(BF16)


def _attn_b(bound, q, k, v, lam_vecs, g_sub):
    b, s, _ = q.shape
    whole = pl.BlockSpec((1, s, B_WIDTH), lambda bi, i: (bi, 0, 0))
    tile = pl.BlockSpec((1, B_Q_TILE, B_WIDTH), lambda bi, i: (bi, i, 0))
    stat = pltpu.VMEM((B_HEADS, 2 * B_Q_TILE, LANES), F32)
    return pl.pallas_call(
        _attn_b_kernel,
        grid=(b, s // B_Q_TILE),
        in_specs=[pl.BlockSpec(memory_space=pltpu.SMEM), tile, whole, whole,
                  _resident(lam_vecs.shape), _resident(g_sub.shape)],
        out_specs=tile,
        out_shape=jax.ShapeDtypeStruct(q.shape, BF16),
        scratch_shapes=[pltpu.VMEM((B_HEADS, 2 * B_Q_TILE, LANES), BF16), stat, stat, stat],
        compiler_params=pltpu.CompilerParams(
            dimension_semantics=("arbitrary", "arbitrary"), vmem_limit_bytes=VMEM_LIMIT),
        name="attn_b",
    )(bound, q, k, v, lam_vecs, g_sub)


def _post_kernel(x_ref, oa_ref, ob_ref, gate_ref, wa_ref, wb_ref, wo_ref,
                 g2_ref, wgu_ref, wdn_ref, gf_ref, o_ref, acc_ref):
    ya = jnp.dot(oa_ref[...], wa_ref[...], preferred_element_type=F32)
    yb = jnp.dot(ob_ref[...], wb_ref[...], preferred_element_type=F32)
    y = (jax.nn.sigmoid(gate_ref[:, :D_MODEL]) * ya
         + jax.nn.sigmoid(gate_ref[:, D_MODEL:]) * yb)
    x2 = x_ref[...] + jnp.dot(y.astype(BF16), wo_ref[...], preferred_element_type=F32)
    _swiglu(_rms(x2, g2_ref[...]).astype(BF16), wgu_ref, wdn_ref, acc_ref)
    o_ref[...] = _rms(x2 + 0.5 * acc_ref[...], gf_ref[...])


def _post(x1, oa, ob, gates, wa, wb, wo, g2, wgu, wdn, gf):
    t = x1.shape[0]
    row = lambda w: pl.BlockSpec((TOKEN_TILE, w), lambda i: (i, 0))
    vec = _resident((1, D_MODEL))
    return pl.pallas_call(
        _post_kernel,
        grid=(t // TOKEN_TILE,),
        in_specs=[row(D_MODEL), row(A_WIDTH), row(B_WIDTH), row(2 * D_MODEL),
                  _resident(wa.shape), _resident(wb.shape), _resident(wo.shape),
                  vec, _resident(wgu.shape), _resident(wdn.shape), vec],
        out_specs=row(D_MODEL),
        out_shape=jax.ShapeDtypeStruct(x1.shape, F32),
        scratch_shapes=[pltpu.VMEM((TOKEN_TILE, D_MODEL), F32)],
        compiler_params=pltpu.CompilerParams(
            dimension_semantics=("arbitrary",), vmem_limit_bytes=VMEM_LIMIT),
        name="post",
    )(x1, oa, ob, gates, wa, wb, wo, g2, wgu, wdn, gf)


def _rotary_tables(seq):
    half = ROT_DIM // 2
    pos = jnp.arange(seq, dtype=F32)
    inv = ROPE_THETA ** (-jnp.arange(0, ROT_DIM, 2, dtype=F32) / ROT_DIM)
    ang = pos[:, None] * inv[None, :]
    cos, sin = jnp.cos(ang), jnp.sin(ang)
    ones = jnp.ones((seq, HEAD_DIM - ROT_DIM), F32)
    zeros = jnp.zeros((seq, HEAD_DIM - half), F32)
    cos_h = jnp.concatenate([cos, cos, ones], axis=-1)
    sin_lo = jnp.concatenate([-sin, zeros], axis=-1)
    sin_hi = jnp.concatenate([jnp.zeros((seq, half), F32), sin,
                              jnp.zeros((seq, HEAD_DIM - ROT_DIM), F32)], axis=-1)
    two = lambda a: jnp.concatenate([a, a], axis=-1)
    return two(cos_h), two(sin_lo), two(sin_hi)


def kernel(x, g_ffn1, w_ffn1_gu, w_ffn1_down, g_mix, w_in, qn_a, kn_a, rel_bias, qn_b, kn_b,
           lambda_q1, lambda_k1, lambda_q2, lambda_k2, g_subln, w_up_a, w_up_b, w_out, g_ffn2,
           w_ffn2_gu, w_ffn2_down, g_final):
    b, s, d = x.shape
    assert d == D_MODEL and s % TOKEN_TILE == 0 and g_ffn1.shape[0] == 1
    l = 0
    t = b * s
    two = lambda g: jnp.concatenate([g, g], axis=-1)[None, :]
    cos, sin_lo, sin_hi = _rotary_tables(s)

    x1 = _ffn1(x.reshape(t, d), g_ffn1[l][None], w_ffn1_gu[l].astype(BF16),
               w_ffn1_down[l].astype(BF16))
    qa, ka, va, qb, kb, vb, gates = _in_proj(
        x1, g_mix[l][None], w_in[l].astype(BF16), two(qn_a[l]), two(kn_a[l]), two(qn_b[l]),
        two(kn_b[l]), cos, sin_lo, sin_hi, s)
    seq3 = lambda a: a.reshape(b, s, a.shape[-1])
    oa = _attn_a(seq3(qa), seq3(ka), seq3(va), _clipped_table(rel_bias[l]))
    lam_vecs = jnp.stack([lambda_q1[l], lambda_k1[l], lambda_q2[l], lambda_k2[l]]).astype(F32)
    bound = (1.02 * math.sqrt(HEAD_DIM) * jnp.max(jnp.abs(qn_b[l])) * jnp.max(jnp.abs(kn_b[l])))
    ob = _attn_b(bound.reshape(1).astype(F32), seq3(qb), seq3(kb), seq3(vb), lam_vecs,
                 g_subln[l][None])
    out = _post(x1, oa.reshape(t, A_WIDTH), ob.reshape(t, B_WIDTH), gates,
                w_up_a[l].astype(BF16), w_up_b[l].astype(BF16), w_out[l].astype(BF16),
                g_ffn2[l][None], w_ffn2_gu[l].astype(BF16), w_ffn2_down[l].astype(BF16),
                g_final[l][None])
    return out.reshape(b, s, d)
```

```python
import functools
import math

import jax
import jax.numpy as jnp
from jax import lax
from jax.experimental import pallas as pl
from jax.experimental.pallas import tpu as pltpu

F32 = jnp.float32
BF16 = jnp.bfloat16

D_MODEL = 1024
D_FF = 2816
CHUNK = 64
LEFT_CHUNKS = 8
HEAD_DIM = 64
A_HEADS = 8
A_WIDTH = A_HEADS * HEAD_DIM
REL_CLIP = 256
B_HEADS = 4
B_WIDTH = B_HEADS * 2 * HEAD_DIM
ROPE_THETA = 500000.0
ROT_DIM = HEAD_DIM // 4
EPS = 1e-6
NEG = -1e30
IN_WIDTH = 3 * A_WIDTH + 3 * B_WIDTH + 2 * D_MODEL
LAMBDA_INIT = 0.8 - 0.6 * math.exp(-0.3 * 0)

LANES = 128
TOKEN_TILE = 512
FF_TILE = 256
A_Q_TILE = 256
A_BAND = A_Q_TILE + LEFT_CHUNKS * CHUNK
A_STRIP_ROWS = 128
A_BIAS_W = 2 * LEFT_CHUNKS * CHUNK + A_BAND
A_TABLE_W = 2048
assert A_TABLE_W >= A_BIAS_W + A_STRIP_ROWS - 1 and A_Q_TILE % A_STRIP_ROWS == 0
B_Q_TILE = 256
B_K_TILE = 2 * B_Q_TILE
SHIFT_SAFE = 40.0
VMEM_LIMIT = 56 * 1024 * 1024


def _resident(shape):
    return pl.BlockSpec(shape, lambda *_: (0,) * len(shape), pipeline_mode=pl.Buffered(1))


def _rms(x, g):
    ms = jnp.mean(x * x, axis=-1, keepdims=True)
    return x * lax.rsqrt(ms + EPS) * g


def _swiglu(xn, wgu_ref, wdn_ref, acc_ref):
    for c in range(D_FF // FF_TILE):
        lo = c * FF_TILE
        g = jnp.dot(xn, wgu_ref[:, lo:lo + FF_TILE], preferred_element_type=F32)
        u = jnp.dot(xn, wgu_ref[:, D_FF + lo:D_FF + lo + FF_TILE], preferred_element_type=F32)
        a = (g * jax.nn.sigmoid(g) * u).astype(BF16)
        d = jnp.dot(a, wdn_ref[lo:lo + FF_TILE, :], preferred_element_type=F32)
        if c == 0:
            acc_ref[...] = d
        else:
            acc_ref[...] += d


def _ffn1_kernel(x_ref, g_ref, wgu_ref, wdn_ref, o_ref, acc_ref):
    x = x_ref[...]
    _swiglu(_rms(x, g_ref[...]).astype(BF16), wgu_ref, wdn_ref, acc_ref)
    o_ref[...] = x + 0.5 * acc_ref[...]


def _ffn1(x, g, wgu, wdn):
    t = x.shape[0]
    row = pl.BlockSpec((TOKEN_TILE, D_MODEL), lambda i: (i, 0))
    return pl.pallas_call(
        _ffn1_kernel,
        grid=(t // TOKEN_TILE,),
        in_specs=[row, _resident((1, D_MODEL)), _resident(wgu.shape), _resident(wdn.shape)],
        out_specs=row,
        out_shape=jax.ShapeDtypeStruct(x.shape, F32),
        scratch_shapes=[pltpu.VMEM((TOKEN_TILE, D_MODEL), F32)],
        compiler_params=pltpu.CompilerParams(
            dimension_semantics=("arbitrary",), vmem_limit_bytes=VMEM_LIMIT),
        name="ffn1",
    )(x, g, wgu, wdn)


def _head_norm(x, g2):
    lane = lax.broadcasted_iota(jnp.int32, x.shape, 1)
    lo = lane < HEAD_DIM
    sq = x * x
    s_lo = jnp.sum(jnp.where(lo, sq, 0.0), axis=-1, keepdims=True)
    s_hi = jnp.sum(jnp.where(lo, 0.0, sq), axis=-1, keepdims=True)
    ms = jnp.where(lo, s_lo, s_hi) * (1.0 / HEAD_DIM)
    return x * lax.rsqrt(ms + EPS) * g2


def _rotary(x, cos, sin_lo, sin_hi):
    half = ROT_DIM // 2
    up = pltpu.roll(x, LANES - half, 1)
    dn = pltpu.roll(x, half, 1)
    return x * cos + up * sin_lo + dn * sin_hi


def _in_proj_kernel(x_ref, g_ref, w_ref, qna_ref, kna_ref, qnb_ref, knb_ref,
                    cos_ref, sl_ref, sh_ref,
                    qa_ref, ka_ref, va_ref, qb_ref, kb_ref, vb_ref, gate_ref):
    h = _rms(x_ref[...], g_ref[...]).astype(BF16)
    scale = 1.0 / math.sqrt(HEAD_DIM)

    def proj(col, width):
        return jnp.dot(h, w_ref[:, col:col + width], preferred_element_type=F32)

    def normed(col, gain_ref, out_ref, rotate, mult):
        y = proj(col, A_WIDTH)
        for p in range(A_WIDTH // LANES):
            t = _head_norm(y[:, p * LANES:(p + 1) * LANES], gain_ref[...])
            if rotate:
                t = _rotary(t, cos_ref[...], sl_ref[...], sh_ref[...])
            if mult != 1.0:
                t = t * mult
            out_ref[:, p * LANES:(p + 1) * LANES] = t.astype(BF16)

    normed(0 * A_WIDTH, qna_ref, qa_ref, False, scale)
    normed(1 * A_WIDTH, kna_ref, ka_ref, False, 1.0)
    va_ref[...] = proj(2 * A_WIDTH, A_WIDTH).astype(BF16)
    normed(3 * A_WIDTH, qnb_ref, qb_ref, True, scale)
    normed(4 * A_WIDTH, knb_ref, kb_ref, True, 1.0)
    vb_ref[...] = proj(5 * A_WIDTH, B_WIDTH).astype(BF16)
    for c in range(2 * D_MODEL // A_WIDTH):
        gate_ref[:, c * A_WIDTH:(c + 1) * A_WIDTH] = proj(6 * A_WIDTH + c * A_WIDTH, A_WIDTH)


def _in_proj(x1, g, w_in, qna, kna, qnb, knb, cos, sin_lo, sin_hi, seq):
    t = x1.shape[0]
    tiles_per_seq = seq // TOKEN_TILE
    row = lambda w: pl.BlockSpec((TOKEN_TILE, w), lambda i: (i, 0))
    pos = pl.BlockSpec((TOKEN_TILE, LANES), lambda i: (i % tiles_per_seq, 0))
    gain = _resident((1, LANES))
    half = jax.ShapeDtypeStruct((t, A_WIDTH), BF16)
    return pl.pallas_call(
        _in_proj_kernel,
        grid=(t // TOKEN_TILE,),
        in_specs=[row(D_MODEL), _resident((1, D_MODEL)), _resident(w_in.shape),
                  gain, gain, gain, gain, pos, pos, pos],
        out_specs=[row(A_WIDTH)] * 6 + [row(2 * D_MODEL)],
        out_shape=[half] * 6 + [jax.ShapeDtypeStruct((t, 2 * D_MODEL), F32)],
        compiler_params=pltpu.CompilerParams(
            dimension_semantics=("arbitrary",), vmem_limit_bytes=VMEM_LIMIT),
        name="in_proj",
    )(x1, g, w_in, qna, kna, qnb, knb, cos, sin_lo, sin_hi)


def _fill_band_bias(f_ref, bias_ref, shift):
    pad = LEFT_CHUNKS * CHUNK
    shape = (A_STRIP_ROWS, A_BIAS_W)
    q_chunk = lax.broadcasted_iota(jnp.int32, shape, 0) >> 6
    k_chunk = (lax.broadcasted_iota(jnp.int32, shape, 1) - 2 * pad) >> 6
    valid = jnp.logical_and(k_chunk <= q_chunk, k_chunk >= q_chunk - LEFT_CHUNKS)
    for h in range(A_HEADS):
        row = jnp.broadcast_to(f_ref[h:h + 1, :], (A_STRIP_ROWS, A_TABLE_W))
        skew = pltpu.roll(row, 0, 1, stride=1, stride_axis=0)
        bias_ref[h] = jnp.where(valid, skew[:, :A_BIAS_W] - shift, NEG)


def _attn_a_kernel(bound_ref, q_ref, k_ref, v_ref, f_ref, o_ref, bias_ref):
    i = pl.program_id(1)
    pad = LEFT_CHUNKS * CHUNK
    bound = bound_ref[0]
    shift_by_bound = bound <= SHIFT_SAFE

    @pl.when(jnp.logical_and(pl.program_id(0) == 0, i == 0))
    def _():
        _fill_band_bias(f_ref, bias_ref, jnp.where(shift_by_bound, bound, 0.0))

    start = pl.multiple_of(jnp.maximum(i * A_Q_TILE - pad, 0), A_Q_TILE)
    off = 2 * pad - i * A_Q_TILE + start
    lane = lax.broadcasted_iota(jnp.int32, (A_Q_TILE, LANES), 1)
    first = lane < HEAD_DIM

    def tile(shifted):
        for p in range(A_WIDTH // LANES):
            cols = slice(p * LANES, (p + 1) * LANES)
            q = q_ref[0, :, cols]
            zero = jnp.zeros_like(q)
            qs = jnp.concatenate([jnp.where(first, q, zero), jnp.where(first, zero, q)], axis=0)
            k = k_ref[0, pl.ds(start, A_BAND), cols]
            v = v_ref[0, pl.ds(start, A_BAND), cols]
            s = lax.dot_general(qs, k, (((1,), (1,)), ((), ())), preferred_element_type=F32)
            bias = [bias_ref[2 * p + e, :,
                             pl.ds(pl.multiple_of(off - t * A_STRIP_ROWS, LANES), A_BAND)]
                    for e in range(2) for t in range(A_Q_TILE // A_STRIP_ROWS)]
            s = s + jnp.concatenate(bias, axis=0)
            if not shifted:
                s = s - jnp.max(s, axis=-1, keepdims=True)
            pr = jnp.exp(s)
            l = jnp.sum(pr, axis=-1, keepdims=True)
            o = jnp.dot(pr.astype(BF16), v, preferred_element_type=F32) / l
            o_ref[0, :, cols] = jnp.where(first, o[:A_Q_TILE], o[A_Q_TILE:]).astype(BF16)

    pl.when(shift_by_bound)(functools.partial(tile, True))
    pl.when(jnp.logical_not(shift_by_bound))(functools.partial(tile, False))


def _attn_a(bound, q, k, v, f_table):
    b, s, _ = q.shape
    whole = pl.BlockSpec((1, s, A_WIDTH), lambda bi, i: (bi, 0, 0))
    tile = pl.BlockSpec((1, A_Q_TILE, A_WIDTH), lambda bi, i: (bi, i, 0))
    return pl.pallas_call(
        _attn_a_kernel,
        grid=(b, s // A_Q_TILE),
        in_specs=[pl.BlockSpec(memory_space=pltpu.SMEM), tile, whole, whole,
                  _resident(f_table.shape)],
        out_specs=tile,
        out_shape=jax.ShapeDtypeStruct(q.shape, BF16),
        scratch_shapes=[pltpu.VMEM((A_HEADS, A_STRIP_ROWS, A_BIAS_W), F32)],
        compiler_params=pltpu.CompilerParams(
            dimension_semantics=("arbitrary", "arbitrary"), vmem_limit_bytes=VMEM_LIMIT),
        name="attn_a",
    )(bound, q, k, v, f_table)


def _clipped_table(rel_table):
    pad = LEFT_CHUNKS * CHUNK
    first, last = rel_table[:1], rel_table[-1:]
    lead = 2 * pad - REL_CLIP
    tail = A_STRIP_ROWS - 1
    mid = A_TABLE_W - lead - rel_table.shape[0] - tail
    f = jnp.concatenate([jnp.repeat(last, lead, axis=0), rel_table[::-1],
                         jnp.repeat(first, mid, axis=0), jnp.repeat(last, tail, axis=0)], axis=0)
    return f.T.astype(F32)


def _attn_b_kernel(bound_ref, q_ref, k_ref, v_ref, lam_ref, gs_ref, o_ref,
                   qs_ref, m_ref, l_ref, acc_ref):
    i = pl.program_id(1)
    nn = (((1,), (1,)), ((), ()))
    lane = lax.broadcasted_iota(jnp.int32, (B_Q_TILE, LANES), 1)
    first = lane < HEAD_DIM
    for h in range(B_HEADS):
        q = q_ref[0, :, h * LANES:(h + 1) * LANES]
        zero = jnp.zeros_like(q)
        qs_ref[h, :B_Q_TILE] = jnp.where(first, q, zero)
        qs_ref[h, B_Q_TILE:] = jnp.where(first, zero, q)
    l_ref[...] = jnp.zeros_like(l_ref)
    acc_ref[...] = jnp.zeros_like(acc_ref)

    def scores(h, lo, width, diag_col):
        k = k_ref[0, pl.ds(lo, width), h * LANES:(h + 1) * LANES]
        s = lax.dot_general(qs_ref[h], k, nn, preferred_element_type=F32)
        if diag_col is not None:
            row = lax.broadcasted_iota(jnp.int32, s.shape, 0) & (B_Q_TILE - 1)
            col = lax.broadcasted_iota(jnp.int32, s.shape, 1) - diag_col
            s = jnp.where((col >> 6) <= (row >> 6), s, NEG)
        return [s[:, c * LANES:(c + 1) * LANES] for c in range(width // LANES)]

    def max_block(lo, width, diag_col):
        for h in range(B_HEADS):
            m_ref[h] = functools.reduce(jnp.maximum, scores(h, lo, width, diag_col), m_ref[h])

    def sum_block(lo, width, diag_col):
        for h in range(B_HEADS):
            m = m_ref[h]
            ps = [jnp.exp(s - m) for s in scores(h, lo, width, diag_col)]
            l_ref[h] += functools.reduce(jnp.add, ps)
            v = v_ref[0, pl.ds(lo, width), h * LANES:(h + 1) * LANES]
            acc_ref[h] += jnp.dot(jnp.concatenate(ps, axis=1).astype(BF16), v,
                                  preferred_element_type=F32)

    def sweep(block):
        def body(j, carry):
            block(pl.multiple_of(j * B_K_TILE, B_K_TILE), B_K_TILE, None)
            return carry
        steps = (i * B_Q_TILE) // B_K_TILE
        lax.fori_loop(0, steps, body, 0)
        lo = pl.multiple_of(steps * B_K_TILE, B_K_TILE)
        odd = (i & 1) == 1
        pl.when(odd)(lambda: block(lo, B_K_TILE, B_K_TILE - B_Q_TILE))
        pl.when(jnp.logical_not(odd))(lambda: block(lo, B_Q_TILE, 0))

    bound = bound_ref[0]
    shift_by_bound = bound <= SHIFT_SAFE

    @pl.when(shift_by_bound)
    def _():
        m_ref[...] = jnp.full(m_ref.shape, bound, F32)

    @pl.when(jnp.logical_not(shift_by_bound))
    def _():
        m_ref[...] = jnp.full(m_ref.shape, NEG, F32)
        sweep(max_block)
        for h in range(B_HEADS):
            m_ref[h] = jnp.broadcast_to(jnp.max(m_ref[h], axis=-1, keepdims=True), m_ref.shape[1:])

    sweep(sum_block)

    lq1, lk1, lq2, lk2 = (lam_ref[n:n + 1, :] for n in range(4))
    lam = (jnp.exp(jnp.sum(lq1 * lk1, axis=-1, keepdims=True))
           - jnp.exp(jnp.sum(lq2 * lk2, axis=-1, keepdims=True)) + LAMBDA_INIT)
    for h in range(B_HEADS):
        o = acc_ref[h] / jnp.sum(l_ref[h], axis=-1, keepdims=True)
        o = o[:B_Q_TILE] - lam * o[B_Q_TILE:]
        o_ref[0, :, h * LANES:(h + 1) * LANES] = (
            _rms(o, gs_ref[...]) * (1.0 - LAMBDA_INIT)).astype(BF16)


def _attn_b(bound, q, k, v, lam_vecs, g_sub):
    b, s, _ = q.shape
    whole = pl.BlockSpec((1, s, B_WIDTH), lambda bi, i: (bi, 0, 0))
    tile = pl.BlockSpec((1, B_Q_TILE, B_WIDTH), lambda bi, i: (bi, i, 0))
    stat = pltpu.VMEM((B_HEADS, 2 * B_Q_TILE, LANES), F32)
    return pl.pallas_call(
        _attn_b_kernel,
        grid=(b, s // B_Q_TILE),
        in_specs=[pl.BlockSpec(memory_space=pltpu.SMEM), tile, whole, whole,
                  _resident(lam_vecs.shape), _resident(g_sub.shape)],
        out_specs=tile,
        out_shape=jax.ShapeDtypeStruct(q.shape, BF16),
        scratch_shapes=[pltpu.VMEM((B_HEADS, 2 * B_Q_TILE, LANES), BF16), stat, stat, stat],
        compiler_params=pltpu.CompilerParams(
            dimension_semantics=("arbitrary", "arbitrary"), vmem_limit_bytes=VMEM_LIMIT),
        name="attn_b",
    )(bound, q, k, v, lam_vecs, g_sub)


def _post_kernel(x_ref, oa_ref, ob_ref, gate_ref, wa_ref, wb_ref, wo_ref,
                 g2_ref, wgu_ref, wdn_ref, gf_ref, o_ref, acc_ref):
    ya = jnp.dot(oa_ref[...], wa_ref[...], preferred_element_type=F32)
    yb = jnp.dot(ob_ref[...], wb_ref[...], preferred_element_type=F32)
    y = (jax.nn.sigmoid(gate_ref[:, :D_MODEL]) * ya
         + jax.nn.sigmoid(gate_ref[:, D_MODEL:]) * yb)
    x2 = x_ref[...] + jnp.dot(y.astype(BF16), wo_ref[...], preferred_element_type=F32)
    _swiglu(_rms(x2, g2_ref[...]).astype(BF16), wgu_ref, wdn_ref, acc_ref)
    o_ref[...] = _rms(x2 + 0.5 * acc_ref[...], gf_ref[...])


def _post(x1, oa, ob, gates, wa, wb, wo, g2, wgu, wdn, gf):
    t = x1.shape[0]
    row = lambda w: pl.BlockSpec((TOKEN_TILE, w), lambda i: (i, 0))
    vec = _resident((1, D_MODEL))
    return pl.pallas_call(
        _post_kernel,
        grid=(t // TOKEN_TILE,),
        in_specs=[row(D_MODEL), row(A_WIDTH), row(B_WIDTH), row(2 * D_MODEL),
                  _resident(wa.shape), _resident(wb.shape), _resident(wo.shape),
                  vec, _resident(wgu.shape), _resident(wdn.shape), vec],
        out_specs=row(D_MODEL),
        out_shape=jax.ShapeDtypeStruct(x1.shape, F32),
        scratch_shapes=[pltpu.VMEM((TOKEN_TILE, D_MODEL), F32)],
        compiler_params=pltpu.CompilerParams(
            dimension_semantics=("arbitrary",), vmem_limit_bytes=VMEM_LIMIT),
        name="post",
    )(x1, oa, ob, gates, wa, wb, wo, g2, wgu, wdn, gf)


def _rotary_tables(seq):
    half = ROT_DIM // 2
    pos = jnp.arange(seq, dtype=F32)
    inv = ROPE_THETA ** (-jnp.arange(0, ROT_DIM, 2, dtype=F32) / ROT_DIM)
    ang = pos[:, None] * inv[None, :]
    cos, sin = jnp.cos(ang), jnp.sin(ang)
    ones = jnp.ones((seq, HEAD_DIM - ROT_DIM), F32)
    zeros = jnp.zeros((seq, HEAD_DIM - half), F32)
    cos_h = jnp.concatenate([cos, cos, ones], axis=-1)
    sin_lo = jnp.concatenate([-sin, zeros], axis=-1)
    sin_hi = jnp.concatenate([jnp.zeros((seq, half), F32), sin,
                              jnp.zeros((seq, HEAD_DIM - ROT_DIM), F32)], axis=-1)
    two = lambda a: jnp.concatenate([a, a], axis=-1)
    return two(cos_h), two(sin_lo), two(sin_hi)


def kernel(x, g_ffn1, w_ffn1_gu, w_ffn1_down, g_mix, w_in, qn_a, kn_a, rel_bias, qn_b, kn_b,
           lambda_q1, lambda_k1, lambda_q2, lambda_k2, g_subln, w_up_a, w_up_b, w_out, g_ffn2,
           w_ffn2_gu, w_ffn2_down, g_final):
    b, s, d = x.shape
    assert d == D_MODEL and s % TOKEN_TILE == 0 and g_ffn1.shape[0] == 1
    l = 0
    t = b * s
    two = lambda g: jnp.concatenate([g, g], axis=-1)[None, :]
    cos, sin_lo, sin_hi = _rotary_tables(s)

    x1 = _ffn1(x.reshape(t, d), g_ffn1[l][None], w_ffn1_gu[l].astype(BF16),
               w_ffn1_down[l].astype(BF16))
    qa, ka, va, qb, kb, vb, gates = _in_proj(
        x1, g_mix[l][None], w_in[l].astype(BF16), two(qn_a[l]), two(kn_a[l]), two(qn_b[l]),
        two(kn_b[l]), cos, sin_lo, sin_hi, s)
    seq3 = lambda a: a.reshape(b, s, a.shape[-1])
    qk_bound = lambda gq, gk: (1.02 * math.sqrt(HEAD_DIM) * jnp.max(jnp.abs(gq))
                               * jnp.max(jnp.abs(gk))).reshape(1).astype(F32)
    bound_a = qk_bound(qn_a[l], kn_a[l]) + jnp.max(jnp.abs(rel_bias[l])).astype(F32)
    oa = _attn_a(bound_a, seq3(qa), seq3(ka), seq3(va), _clipped_table(rel_bias[l]))
    lam_vecs = jnp.stack([lambda_q1[l], lambda_k1[l], lambda_q2[l], lambda_k2[l]]).astype(F32)
    ob = _attn_b(qk_bound(qn_b[l], kn_b[l]), seq3(qb), seq3(kb), seq3(vb), lam_vecs,
                 g_subln[l][None])
    out = _post(x1, oa.reshape(t, A_WIDTH), ob.reshape(t, B_WIDTH), gates,
                w_up_a[l].astype(BF16), w_up_b[l].astype(BF16), w_out[l].astype(BF16),
                g_ffn2[l][None], w_ffn2_gu[l].astype(BF16), w_ffn2_down[l].astype(BF16),
                g_final[l][None])
    return out.reshape(b, s, d)
```

```python
import functools
import math

import jax
import jax.numpy as jnp
from jax import lax
from jax.experimental import pallas as pl
from jax.experimental.pallas import tpu as pltpu

F32 = jnp.float32
BF16 = jnp.bfloat16

D_MODEL = 1024
D_FF = 2816
CHUNK = 64
LEFT_CHUNKS = 8
HEAD_DIM = 64
A_HEADS = 8
A_WIDTH = A_HEADS * HEAD_DIM
REL_CLIP = 256
B_HEADS = 4
B_WIDTH = B_HEADS * 2 * HEAD_DIM
ROPE_THETA = 500000.0
ROT_DIM = HEAD_DIM // 4
EPS = 1e-6
NEG = -1e30
IN_WIDTH = 3 * A_WIDTH + 3 * B_WIDTH + 2 * D_MODEL
LAMBDA_INIT = 0.8 - 0.6 * math.exp(-0.3 * 0)

LANES = 128
TOKEN_TILE = 512
FF_TILE = 256
A_Q_TILE = 256
A_BAND = A_Q_TILE + LEFT_CHUNKS * CHUNK
A_STRIP_ROWS = 128
A_BIAS_W = 2 * LEFT_CHUNKS * CHUNK + A_BAND
A_TABLE_W = 2048
assert A_TABLE_W >= A_BIAS_W + A_STRIP_ROWS - 1 and A_Q_TILE % A_STRIP_ROWS == 0
B_Q_TILE = 256
B_K_TILE = 2 * B_Q_TILE
SHIFT_SAFE = 40.0
VMEM_LIMIT = 56 * 1024 * 1024


def _resident(shape):
    return pl.BlockSpec(shape, lambda *_: (0,) * len(shape), pipeline_mode=pl.Buffered(1))


def _rms(x, g):
    ms = jnp.mean(x * x, axis=-1, keepdims=True)
    return x * lax.rsqrt(ms + EPS) * g


def _swiglu(xn, wgu_ref, wdn_ref, acc_ref):
    for c in range(D_FF // FF_TILE):
        lo = c * FF_TILE
        g = jnp.dot(xn, wgu_ref[:, lo:lo + FF_TILE], preferred_element_type=F32)
        u = jnp.dot(xn, wgu_ref[:, D_FF + lo:D_FF + lo + FF_TILE], preferred_element_type=F32)
        a = (g * jax.nn.sigmoid(g) * u).astype(BF16)
        d = jnp.dot(a, wdn_ref[lo:lo + FF_TILE, :], preferred_element_type=F32)
        if c == 0:
            acc_ref[...] = d
        else:
            acc_ref[...] += d


def _ffn1_kernel(x_ref, g_ref, wgu_ref, wdn_ref, o_ref, acc_ref):
    x = x_ref[...]
    _swiglu(_rms(x, g_ref[...]).astype(BF16), wgu_ref, wdn_ref, acc_ref)
    o_ref[...] = x + 0.5 * acc_ref[...]


def _ffn1(x, g, wgu, wdn):
    t = x.shape[0]
    row = pl.BlockSpec((TOKEN_TILE, D_MODEL), lambda i: (i, 0))
    return pl.pallas_call(
        _ffn1_kernel,
        grid=(t // TOKEN_TILE,),
        in_specs=[row, _resident((1, D_MODEL)), _resident(wgu.shape), _resident(wdn.shape)],
        out_specs=row,
        out_shape=jax.ShapeDtypeStruct(x.shape, F32),
        scratch_shapes=[pltpu.VMEM((TOKEN_TILE, D_MODEL), F32)],
        compiler_params=pltpu.CompilerParams(
            dimension_semantics=("arbitrary",), vmem_limit_bytes=VMEM_LIMIT),
        name="ffn1",
    )(x, g, wgu, wdn)


def _head_norm(x, g2):
    lane = lax.broadcasted_iota(jnp.int32, x.shape, 1)
    lo = lane < HEAD_DIM
    sq = x * x
    s_lo = jnp.sum(jnp.where(lo, sq, 0.0), axis=-1, keepdims=True)
    s_hi = jnp.sum(jnp.where(lo, 0.0, sq), axis=-1, keepdims=True)
    ms = jnp.where(lo, s_lo, s_hi) * (1.0 / HEAD_DIM)
    return x * lax.rsqrt(ms + EPS) * g2


def _rotary(x, cos, sin_lo, sin_hi):
    half = ROT_DIM // 2
    up = pltpu.roll(x, LANES - half, 1)
    dn = pltpu.roll(x, half, 1)
    return x * cos + up * sin_lo + dn * sin_hi


def _in_proj_kernel(x_ref, g_ref, w_ref, qna_ref, kna_ref, qnb_ref, knb_ref,
                    cos_ref, sl_ref, sh_ref,
                    qa_ref, ka_ref, va_ref, qb_ref, kb_ref, vb_ref, gate_ref):
    h = _rms(x_ref[...], g_ref[...]).astype(BF16)
    scale = 1.0 / math.sqrt(HEAD_DIM)

    def proj(col, width):
        return jnp.dot(h, w_ref[:, col:col + width], preferred_element_type=F32)

    def normed(col, gain_ref, out_ref, rotate, mult):
        y = proj(col, A_WIDTH)
        for p in range(A_WIDTH // LANES):
            t = _head_norm(y[:, p * LANES:(p + 1) * LANES], gain_ref[...])
            if rotate:
                t = _rotary(t, cos_ref[...], sl_ref[...], sh_ref[...])
            if mult != 1.0:
                t = t * mult
            out_ref[:, p * LANES:(p + 1) * LANES] = t.astype(BF16)

    normed(0 * A_WIDTH, qna_ref, qa_ref, False, scale)
    normed(1 * A_WIDTH, kna_ref, ka_ref, False, 1.0)
    va_ref[...] = proj(2 * A_WIDTH, A_WIDTH).astype(BF16)
    normed(3 * A_WIDTH, qnb_ref, qb_ref, True, scale)
    normed(4 * A_WIDTH, knb_ref, kb_ref, True, 1.0)
    vb_ref[...] = proj(5 * A_WIDTH, B_WIDTH).astype(BF16)
    for c in range(2 * D_MODEL // A_WIDTH):
        gate_ref[:, c * A_WIDTH:(c + 1) * A_WIDTH] = proj(6 * A_WIDTH + c * A_WIDTH, A_WIDTH)


def _in_proj(x1, g, w_in, qna, kna, qnb, knb, cos, sin_lo, sin_hi, seq):
    t = x1.shape[0]
    tiles_per_seq = seq // TOKEN_TILE
    row = lambda w: pl.BlockSpec((TOKEN_TILE, w), lambda i: (i, 0))
    pos = pl.BlockSpec((TOKEN_TILE, LANES), lambda i: (i % tiles_per_seq, 0))
    gain = _resident((1, LANES))
    half = jax.ShapeDtypeStruct((t, A_WIDTH), BF16)
    return pl.pallas_call(
        _in_proj_kernel,
        grid=(t // TOKEN_TILE,),
        in_specs=[row(D_MODEL), _resident((1, D_MODEL)), _resident(w_in.shape),
                  gain, gain, gain, gain, pos, pos, pos],
        out_specs=[row(A_WIDTH)] * 6 + [row(2 * D_MODEL)],
        out_shape=[half] * 6 + [jax.ShapeDtypeStruct((t, 2 * D_MODEL), F32)],
        compiler_params=pltpu.CompilerParams(
            dimension_semantics=("arbitrary",), vmem_limit_bytes=VMEM_LIMIT),
        name="in_proj",
    )(x1, g, w_in, qna, kna, qnb, knb, cos, sin_lo, sin_hi)


def _fill_band_bias(f_ref, bias_ref, shift):
    pad = LEFT_CHUNKS * CHUNK
    shape = (A_STRIP_ROWS, A_BIAS_W)
    q_chunk = lax.broadcasted_iota(jnp.int32, shape, 0) >> 6
    k_chunk = (lax.broadcasted_iota(jnp.int32, shape, 1) - 2 * pad) >> 6
    valid = jnp.logical_and(k_chunk <= q_chunk, k_chunk >= q_chunk - LEFT_CHUNKS)
    for h in range(A_HEADS):
        row = jnp.broadcast_to(f_ref[h:h + 1, :], (A_STRIP_ROWS, A_TABLE_W))
        skew = pltpu.roll(row, 0, 1, stride=1, stride_axis=0)
        bias_ref[h] = jnp.where(valid, skew[:, :A_BIAS_W] - shift, NEG)


def _attn_a_kernel(bound_ref, q_ref, k_ref, v_ref, f_ref, o_ref, bias_ref):
    i = pl.program_id(1)
    pad = LEFT_CHUNKS * CHUNK
    bound = bound_ref[0]
    shift_by_bound = bound <= SHIFT_SAFE

    @pl.when(jnp.logical_and(pl.program_id(0) == 0, i == 0))
    def _():
        _fill_band_bias(f_ref, bias_ref, jnp.where(shift_by_bound, bound, 0.0))

    start = pl.multiple_of(jnp.maximum(i * A_Q_TILE - pad, 0), A_Q_TILE)
    off = 2 * pad - i * A_Q_TILE + start
    lane = lax.broadcasted_iota(jnp.int32, (A_Q_TILE, LANES), 1)
    first = lane < HEAD_DIM

    def tile(shifted):
        for p in range(A_WIDTH // LANES):
            cols = slice(p * LANES, (p + 1) * LANES)
            q = q_ref[0, :, cols]
            zero = jnp.zeros_like(q)
            qs = jnp.concatenate([jnp.where(first, q, zero), jnp.where(first, zero, q)], axis=0)
            k = k_ref[0, pl.ds(start, A_BAND), cols]
            v = v_ref[0, pl.ds(start, A_BAND), cols]
            s = lax.dot_general(qs, k, (((1,), (1,)), ((), ())), preferred_element_type=F32)
            bias = [bias_ref[2 * p + e, :,
                             pl.ds(pl.multiple_of(off - t * A_STRIP_ROWS, LANES), A_BAND)]
                    for e in range(2) for t in range(A_Q_TILE // A_STRIP_ROWS)]
            s = s + jnp.concatenate(bias, axis=0)
            if not shifted:
                s = s - jnp.max(s, axis=-1, keepdims=True)
            pr = jnp.exp(s)
            l = jnp.sum(pr, axis=-1, keepdims=True)
            o = jnp.dot(pr.astype(BF16), v, preferred_element_type=F32) / l
            o_ref[0, :, cols] = jnp.where(first, o[:A_Q_TILE], o[A_Q_TILE:]).astype(BF16)

    pl.when(shift_by_bound)(functools.partial(tile, True))
    pl.when(jnp.logical_not(shift_by_bound))(functools.partial(tile, False))


def _attn_a(bound, q, k, v, f_table):
    b, s, _ = q.shape
    whole = pl.BlockSpec((1, s, A_WIDTH), lambda bi, i: (bi, 0, 0))
    tile = pl.BlockSpec((1, A_Q_TILE, A_WIDTH), lambda bi, i: (bi, i, 0))
    return pl.pallas_call(
        _attn_a_kernel,
        grid=(b, s // A_Q_TILE),
        in_specs=[pl.BlockSpec(memory_space=pltpu.SMEM), tile, whole, whole,
                  _resident(f_table.shape)],
        out_specs=tile,
        out_shape=jax.ShapeDtypeStruct(q.shape, BF16),
        scratch_shapes=[pltpu.VMEM((A_HEADS, A_STRIP_ROWS, A_BIAS_W), F32)],
        compiler_params=pltpu.CompilerParams(
            dimension_semantics=("arbitrary", "arbitrary"), vmem_limit_bytes=VMEM_LIMIT),
        name="attn_a",
    )(bound, q, k, v, f_table)


def _clipped_table(rel_table):
    pad = LEFT_CHUNKS * CHUNK
    first, last = rel_table[:1], rel_table[-1:]
    lead = 2 * pad - REL_CLIP
    tail = A_STRIP_ROWS - 1
    mid = A_TABLE_W - lead - rel_table.shape[0] - tail
    f = jnp.concatenate([jnp.repeat(last, lead, axis=0), rel_table[::-1],
                         jnp.repeat(first, mid, axis=0), jnp.repeat(last, tail, axis=0)], axis=0)
    return f.T.astype(F32)


def _attn_b_kernel(bound_ref, q0_ref, q1_ref, k_ref, v_ref, lam_ref, gs_ref, o0_ref, o1_ref,
                   qs_ref, m_ref, l_ref, acc_ref):
    i = pl.program_id(1)
    n_tiles = k_ref.shape[1] // B_Q_TILE
    n_full = n_tiles // 2 - 1
    nn = (((1,), (1,)), ((), ()))
    lane = lax.broadcasted_iota(jnp.int32, (B_Q_TILE, LANES), 1)
    first = lane < HEAD_DIM

    def stack_queries():
        for slot, q_ref in enumerate((q0_ref, q1_ref)):
            for h in range(B_HEADS):
                q = q_ref[0, :, h * LANES:(h + 1) * LANES]
                zero = jnp.zeros_like(q)
                qs_ref[slot, h, :B_Q_TILE] = jnp.where(first, q, zero)
                qs_ref[slot, h, B_Q_TILE:] = jnp.where(first, zero, q)

    i_odd = (i & 1) == 1
    slot_odd = jnp.where(i_odd, 0, 1)
    slot_even = 1 - slot_odd
    tile_odd = jnp.where(i_odd, i, n_tiles - 1 - i)
    tile_even = jnp.where(i_odd, n_tiles - 1 - i, i)
    lo_odd = pl.multiple_of((tile_odd - 1) * B_Q_TILE, B_K_TILE)
    lo_even = pl.multiple_of(tile_even * B_Q_TILE, B_K_TILE)
    steps0 = i >> 1

    def scores(slot, h, lo, width, diag_col):
        k = k_ref[0, pl.ds(lo, width), h * LANES:(h + 1) * LANES]
        s = lax.dot_general(qs_ref[slot, h], k, nn, preferred_element_type=F32)
        if diag_col is not None:
            row = lax.broadcasted_iota(jnp.int32, s.shape, 0) & (B_Q_TILE - 1)
            col = lax.broadcasted_iota(jnp.int32, s.shape, 1) - diag_col
            s = jnp.where((col >> 6) <= (row >> 6), s, NEG)
        return [s[:, c * LANES:(c + 1) * LANES] for c in range(width // LANES)]

    def max_block(slot, lo, width, diag_col, assign):
        for h in range(B_HEADS):
            m = functools.reduce(jnp.maximum, scores(slot, h, lo, width, diag_col))
            m_ref[slot, h] = m if assign else jnp.maximum(m_ref[slot, h], m)

    def sum_block(shift, slot, lo, width, diag_col, assign):
        for h in range(B_HEADS):
            m = m_ref[slot, h] if shift is None else shift
            ps = [jnp.exp(s - m) for s in scores(slot, h, lo, width, diag_col)]
            v = v_ref[0, pl.ds(lo, width), h * LANES:(h + 1) * LANES]
            l = functools.reduce(jnp.add, ps)
            pv = jnp.dot(jnp.concatenate(ps, axis=1).astype(BF16), v, preferred_element_type=F32)
            l_ref[slot, h] = l if assign else l_ref[slot, h] + l
            acc_ref[slot, h] = pv if assign else acc_ref[slot, h] + pv

    def sweep(block, unroll):
        block(slot_odd, lo_odd, B_K_TILE, B_K_TILE - B_Q_TILE, True)
        block(slot_even, lo_even, B_Q_TILE, 0, True)

        def full(t):
            in_slot0 = t < steps0
            lo = jnp.where(in_slot0, t, t - steps0) * B_K_TILE
            block(jnp.where(in_slot0, 0, 1), pl.multiple_of(lo, B_K_TILE), B_K_TILE, None, False)

        if unroll:
            for t in range(n_full):
                full(t)
        else:
            def body(t, carry):
                full(t)
                return carry
            lax.fori_loop(0, n_full, body, 0)

    bound = bound_ref[0]
    shift_by_bound = bound <= SHIFT_SAFE

    def finish():
        lq1, lk1, lq2, lk2 = (lam_ref[n:n + 1, :] for n in range(4))
        lam = (jnp.exp(jnp.sum(lq1 * lk1, axis=-1, keepdims=True))
               - jnp.exp(jnp.sum(lq2 * lk2, axis=-1, keepdims=True)) + LAMBDA_INIT)
        for slot, o_ref in enumerate((o0_ref, o1_ref)):
            for h in range(B_HEADS):
                o = acc_ref[slot, h] / jnp.sum(l_ref[slot, h], axis=-1, keepdims=True)
                o = o[:B_Q_TILE] - lam * o[B_Q_TILE:]
                o_ref[0, :, h * LANES:(h + 1) * LANES] = (
                    _rms(o, gs_ref[...]) * (1.0 - LAMBDA_INIT)).astype(BF16)

    @pl.when(shift_by_bound)
    def _():
        stack_queries()
        sweep(functools.partial(sum_block, bound), True)
        finish()

    @pl.when(jnp.logical_not(shift_by_bound))
    def _():
        stack_queries()
        sweep(max_block, False)
        for slot in range(2):
            for h in range(B_HEADS):
                m_ref[slot, h] = jnp.broadcast_to(
                    jnp.max(m_ref[slot, h], axis=-1, keepdims=True), m_ref.shape[2:])
        sweep(functools.partial(sum_block, None), False)
        finish()


def _attn_b(bound, q, k, v, lam_vecs, g_sub):
    b, s, _ = q.shape
    n_tiles = s // B_Q_TILE
    assert n_tiles % 2 == 0 and B_K_TILE == 2 * B_Q_TILE
    half = n_tiles // 2
    whole = pl.BlockSpec((1, s, B_WIDTH), lambda bi, i: (bi, 0, 0))
    q_lo = pl.BlockSpec((1, B_Q_TILE, B_WIDTH), lambda bi, i: (bi, i, 0))
    q_hi = pl.BlockSpec((1, B_Q_TILE, B_WIDTH), lambda bi, i: (bi, n_tiles - 1 - i, 0))
    o_hi = pl.BlockSpec((1, B_Q_TILE, B_WIDTH), lambda bi, i: (bi, half - 1 - i, 0))
    stat = pltpu.VMEM((2, B_HEADS, 2 * B_Q_TILE, LANES), F32)
    out = jax.ShapeDtypeStruct((b, s // 2, B_WIDTH), BF16)
    return pl.pallas_call(
        _attn_b_kernel,
        grid=(b, half),
        in_specs=[pl.BlockSpec(memory_space=pltpu.SMEM), q_lo, q_hi, whole, whole,
                  _resident(lam_vecs.shape), _resident(g_sub.shape)],
        out_specs=[q_lo, o_hi],
        out_shape=[out, out],
        scratch_shapes=[pltpu.VMEM((2, B_HEADS, 2 * B_Q_TILE, LANES), BF16), stat, stat, stat],
        compiler_params=pltpu.CompilerParams(
            dimension_semantics=("arbitrary", "arbitrary"), vmem_limit_bytes=VMEM_LIMIT),
        name="attn_b",
    )(bound, q, q, k, v, lam_vecs, g_sub)


def _post_kernel(x_ref, oa_ref, ob_lo_ref, ob_hi_ref, gate_ref, wa_ref, wb_ref, wo_ref,
                 g2_ref, wgu_ref, wdn_ref, gf_ref, o_ref, acc_ref, *, tiles_per_seq):
    ya = jnp.dot(oa_ref[...], wa_ref[...], preferred_element_type=F32)
    in_lo = (pl.program_id(0) % tiles_per_seq) < tiles_per_seq // 2
    ob = jnp.where(in_lo, ob_lo_ref[...], ob_hi_ref[...])
    yb = jnp.dot(ob, wb_ref[...], preferred_element_type=F32)
    y = (jax.nn.sigmoid(gate_ref[:, :D_MODEL]) * ya
         + jax.nn.sigmoid(gate_ref[:, D_MODEL:]) * yb)
    x2 = x_ref[...] + jnp.dot(y.astype(BF16), wo_ref[...], preferred_element_type=F32)
    _swiglu(_rms(x2, g2_ref[...]).astype(BF16), wgu_ref, wdn_ref, acc_ref)
    o_ref[...] = _rms(x2 + 0.5 * acc_ref[...], gf_ref[...])


def _post(x1, oa, ob_lo, ob_hi, gates, wa, wb, wo, g2, wgu, wdn, gf, seq):
    t = x1.shape[0]
    tiles_per_seq = seq // TOKEN_TILE
    half = tiles_per_seq // 2
    assert tiles_per_seq % 2 == 0
    row = lambda w: pl.BlockSpec((TOKEN_TILE, w), lambda i: (i, 0))
    lo = pl.BlockSpec((TOKEN_TILE, B_WIDTH), lambda i: (
        (i // tiles_per_seq) * half + jnp.minimum(i % tiles_per_seq, half - 1), 0))
    hi = pl.BlockSpec((TOKEN_TILE, B_WIDTH), lambda i: (
        (i // tiles_per_seq) * half + jnp.maximum(i % tiles_per_seq - half, 0), 0))
    vec = _resident((1, D_MODEL))
    return pl.pallas_call(
        functools.partial(_post_kernel, tiles_per_seq=tiles_per_seq),
        grid=(t // TOKEN_TILE,),
        in_specs=[row(D_MODEL), row(A_WIDTH), lo, hi, row(2 * D_MODEL),
                  _resident(wa.shape), _resident(wb.shape), _resident(wo.shape),
                  vec, _resident(wgu.shape), _resident(wdn.shape), vec],
        out_specs=row(D_MODEL),
        out_shape=jax.ShapeDtypeStruct(x1.shape, F32),
        scratch_shapes=[pltpu.VMEM((TOKEN_TILE, D_MODEL), F32)],
        compiler_params=pltpu.CompilerParams(
            dimension_semantics=("arbitrary",), vmem_limit_bytes=VMEM_LIMIT),
        name="post",
    )(x1, oa, ob_lo, ob_hi, gates, wa, wb, wo, g2, wgu, wdn, gf)


def _rotary_tables(seq):
    half = ROT_DIM // 2
    pos = jnp.arange(seq, dtype=F32)
    inv = ROPE_THETA ** (-jnp.arange(0, ROT_DIM, 2, dtype=F32) / ROT_DIM)
    ang = pos[:, None] * inv[None, :]
    cos, sin = jnp.cos(ang), jnp.sin(ang)
    ones = jnp.ones((seq, HEAD_DIM - ROT_DIM), F32)
    zeros = jnp.zeros((seq, HEAD_DIM - half), F32)
    cos_h = jnp.concatenate([cos, cos, ones], axis=-1)
    sin_lo = jnp.concatenate([-sin, zeros], axis=-1)
    sin_hi = jnp.concatenate([jnp.zeros((seq, half), F32), sin,
                              jnp.zeros((seq, HEAD_DIM - ROT_DIM), F32)], axis=-1)
    two = lambda a: jnp.concatenate([a, a], axis=-1)
    return two(cos_h), two(sin_lo), two(sin_hi)


def kernel(x, g_ffn1, w_ffn1_gu, w_ffn1_down, g_mix, w_in, qn_a, kn_a, rel_bias, qn_b, kn_b,
           lambda_q1, lambda_k1, lambda_q2, lambda_k2, g_subln, w_up_a, w_up_b, w_out, g_ffn2,
           w_ffn2_gu, w_ffn2_down, g_final):
    b, s, d = x.shape
    assert d == D_MODEL and s % TOKEN_TILE == 0 and g_ffn1.shape[0] == 1
    l = 0
    t = b * s
    two = lambda g: jnp.concatenate([g, g], axis=-1)[None, :]
    cos, sin_lo, sin_hi = _rotary_tables(s)

    x1 = _ffn1(x.reshape(t, d), g_ffn1[l][None], w_ffn1_gu[l].astype(BF16),
               w_ffn1_down[l].astype(BF16))
    qa, ka, va, qb, kb, vb, gates = _in_proj(
        x1, g_mix[l][None], w_in[l].astype(BF16), two(qn_a[l]), two(kn_a[l]), two(qn_b[l]),
        two(kn_b[l]), cos, sin_lo, sin_hi, s)
    seq3 = lambda a: a.reshape(b, s, a.shape[-1])
    qk_bound = lambda gq, gk: (1.02 * math.sqrt(HEAD_DIM) * jnp.max(jnp.abs(gq))
                               * jnp.max(jnp.abs(gk))).reshape(1).astype(F32)
    bound_a = qk_bound(qn_a[l], kn_a[l]) + jnp.max(jnp.abs(rel_bias[l])).astype(F32)
    oa = _attn_a(bound_a, seq3(qa), seq3(ka), seq3(va), _clipped_table(rel_bias[l]))
    lam_vecs = jnp.stack([lambda_q1[l], lambda_k1[l], lambda_q2[l], lambda_k2[l]]).astype(F32)
    ob_lo, ob_hi = _attn_b(qk_bound(qn_b[l], kn_b[l]), seq3(qb), seq3(kb), seq3(vb), lam_vecs,
                           g_subln[l][None])
    out = _post(x1, oa.reshape(t, A_WIDTH), ob_lo.reshape(t // 2, B_WIDTH),
                ob_hi.reshape(t // 2, B_WIDTH), gates,
                w_up_a[l].astype(BF16), w_up_b[l].astype(BF16), w_out[l].astype(BF16),
                g_ffn2[l][None], w_ffn2_gu[l].astype(BF16), w_ffn2_down[l].astype(BF16),
                g_final[l][None], s)
    return out.reshape(b, s, d)
```

```python
import functools
import math

import jax
import jax.numpy as jnp
import numpy as np
from jax import lax
from jax.experimental import pallas as pl
from jax.experimental.pallas import tpu as pltpu

F32 = jnp.float32
BF16 = jnp.bfloat16

D_MODEL = 1024
D_FF = 2816
CHUNK = 64
LEFT_CHUNKS = 8
HEAD_DIM = 64
A_HEADS = 8
A_WIDTH = A_HEADS * HEAD_DIM
REL_CLIP = 256
B_HEADS = 4
B_WIDTH = B_HEADS * 2 * HEAD_DIM
ROPE_THETA = 500000.0
ROT_DIM = HEAD_DIM // 4
EPS = 1e-6
NEG = -1e30
IN_WIDTH = 3 * A_WIDTH + 3 * B_WIDTH + 2 * D_MODEL
LAMBDA_INIT = 0.8 - 0.6 * math.exp(-0.3 * 0)

LANES = 128
TOKEN_TILE = 512
FF_TILE = 256
A_Q_TILE = 256
A_BAND = A_Q_TILE + LEFT_CHUNKS * CHUNK
A_STEP_TILES = 2
A_STRIP_ROWS = 128
A_BIAS_W = 2 * LEFT_CHUNKS * CHUNK + A_BAND
A_TABLE_W = 2048
assert A_TABLE_W >= A_BIAS_W + A_STRIP_ROWS - 1 and A_Q_TILE % A_STRIP_ROWS == 0
B_Q_TILE = 256
B_K_TILE = B_Q_TILE
SHIFT_SAFE = 40.0
VMEM_LIMIT = 56 * 1024 * 1024


def _resident(shape):
    return pl.BlockSpec(shape, lambda *_: (0,) * len(shape), pipeline_mode=pl.Buffered(1))


def _rms(x, g):
    ms = jnp.mean(x * x, axis=-1, keepdims=True)
    return x * lax.rsqrt(ms + EPS) * g


def _swiglu(xn, wgu_ref, wdn_ref, acc_ref):
    for c in range(D_FF // FF_TILE):
        lo = c * FF_TILE
        g = jnp.dot(xn, wgu_ref[:, lo:lo + FF_TILE], preferred_element_type=F32)
        u = jnp.dot(xn, wgu_ref[:, D_FF + lo:D_FF + lo + FF_TILE], preferred_element_type=F32)
        a = (g * jax.nn.sigmoid(g) * u).astype(BF16)
        d = jnp.dot(a, wdn_ref[lo:lo + FF_TILE, :], preferred_element_type=F32)
        if c == 0:
            acc_ref[...] = d
        else:
            acc_ref[...] += d


def _ffn1_kernel(x_ref, g_ref, wgu_ref, wdn_ref, o_ref, acc_ref):
    x = x_ref[...]
    _swiglu(_rms(x, g_ref[...]).astype(BF16), wgu_ref, wdn_ref, acc_ref)
    o_ref[...] = x + 0.5 * acc_ref[...]


def _ffn1(x, g, wgu, wdn):
    t = x.shape[0]
    row = pl.BlockSpec((TOKEN_TILE, D_MODEL), lambda i: (i, 0))
    return pl.pallas_call(
        _ffn1_kernel,
        grid=(t // TOKEN_TILE,),
        in_specs=[row, _resident((1, D_MODEL)), _resident(wgu.shape), _resident(wdn.shape)],
        out_specs=row,
        out_shape=jax.ShapeDtypeStruct(x.shape, F32),
        scratch_shapes=[pltpu.VMEM((TOKEN_TILE, D_MODEL), F32)],
        compiler_params=pltpu.CompilerParams(
            dimension_semantics=("arbitrary",), vmem_limit_bytes=VMEM_LIMIT),
        name="ffn1",
    )(x, g, wgu, wdn)


def _head_norm(x, g2):
    lane = lax.broadcasted_iota(jnp.int32, x.shape, 1)
    lo = lane < HEAD_DIM
    sq = x * x
    s_lo = jnp.sum(jnp.where(lo, sq, 0.0), axis=-1, keepdims=True)
    s_hi = jnp.sum(jnp.where(lo, 0.0, sq), axis=-1, keepdims=True)
    ms = jnp.where(lo, s_lo, s_hi) * (1.0 / HEAD_DIM)
    return x * lax.rsqrt(ms + EPS) * g2


def _rotary(x, cos, sin_lo, sin_hi):
    half = ROT_DIM // 2
    up = pltpu.roll(x, LANES - half, 1)
    dn = pltpu.roll(x, half, 1)
    return x * cos + up * sin_lo + dn * sin_hi


def _in_proj_kernel(x_ref, g_ref, w_ref, qna_ref, kna_ref, qnb_ref, knb_ref,
                    cos_ref, sl_ref, sh_ref,
                    qa_ref, ka_ref, va_ref, qb_ref, kb_ref, vb_ref, gate_ref):
    h = _rms(x_ref[...], g_ref[...]).astype(BF16)
    scale = 1.0 / math.sqrt(HEAD_DIM)

    def proj(col, width):
        return jnp.dot(h, w_ref[:, col:col + width], preferred_element_type=F32)

    def normed(col, gain_ref, out_ref, rotate, mult):
        y = proj(col, A_WIDTH)
        for p in range(A_WIDTH // LANES):
            t = _head_norm(y[:, p * LANES:(p + 1) * LANES], gain_ref[...])
            if rotate:
                t = _rotary(t, cos_ref[...], sl_ref[...], sh_ref[...])
            if mult != 1.0:
                t = t * mult
            out_ref[:, p * LANES:(p + 1) * LANES] = t.astype(BF16)

    normed(0 * A_WIDTH, qna_ref, qa_ref, False, scale)
    normed(1 * A_WIDTH, kna_ref, ka_ref, False, 1.0)
    va_ref[...] = proj(2 * A_WIDTH, A_WIDTH).astype(BF16)
    normed(3 * A_WIDTH, qnb_ref, qb_ref, True, scale)
    normed(4 * A_WIDTH, knb_ref, kb_ref, True, 1.0)
    vb_ref[...] = proj(5 * A_WIDTH, B_WIDTH).astype(BF16)
    for c in range(2 * D_MODEL // A_WIDTH):
        gate_ref[:, c * A_WIDTH:(c + 1) * A_WIDTH] = proj(6 * A_WIDTH + c * A_WIDTH, A_WIDTH)


def _in_proj(x1, g, w_in, qna, kna, qnb, knb, cos, sin_lo, sin_hi, seq):
    t = x1.shape[0]
    tiles_per_seq = seq // TOKEN_TILE
    row = lambda w: pl.BlockSpec((TOKEN_TILE, w), lambda i: (i, 0))
    pos = pl.BlockSpec((TOKEN_TILE, LANES), lambda i: (i % tiles_per_seq, 0))
    gain = _resident((1, LANES))
    half = jax.ShapeDtypeStruct((t, A_WIDTH), BF16)
    return pl.pallas_call(
        _in_proj_kernel,
        grid=(t // TOKEN_TILE,),
        in_specs=[row(D_MODEL), _resident((1, D_MODEL)), _resident(w_in.shape),
                  gain, gain, gain, gain, pos, pos, pos],
        out_specs=[row(A_WIDTH)] * 6 + [row(2 * D_MODEL)],
        out_shape=[half] * 6 + [jax.ShapeDtypeStruct((t, 2 * D_MODEL), F32)],
        compiler_params=pltpu.CompilerParams(
            dimension_semantics=("arbitrary",), vmem_limit_bytes=VMEM_LIMIT),
        name="in_proj",
    )(x1, g, w_in, qna, kna, qnb, knb, cos, sin_lo, sin_hi)


def _fill_band_bias(f_ref, bias_ref, shift):
    pad = LEFT_CHUNKS * CHUNK
    shape = (A_STRIP_ROWS, A_BIAS_W)
    q_chunk = lax.broadcasted_iota(jnp.int32, shape, 0) >> 6
    k_chunk = (lax.broadcasted_iota(jnp.int32, shape, 1) - 2 * pad) >> 6
    valid = jnp.logical_and(k_chunk <= q_chunk, k_chunk >= q_chunk - LEFT_CHUNKS)
    for h in range(A_HEADS):
        row = jnp.broadcast_to(f_ref[h:h + 1, :], (A_STRIP_ROWS, A_TABLE_W))
        skew = pltpu.roll(row, 0, 1, stride=1, stride_axis=0)
        bias_ref[h] = jnp.where(valid, skew[:, :A_BIAS_W] - shift, NEG)


def _attn_a_kernel(bound_ref, q_ref, k_ref, v_ref, f_ref, o_ref, bias_ref):
    i = pl.program_id(1)
    pad = LEFT_CHUNKS * CHUNK
    bound = bound_ref[0]
    shift_by_bound = bound <= SHIFT_SAFE

    @pl.when(jnp.logical_and(pl.program_id(0) == 0, i == 0))
    def _():
        _fill_band_bias(f_ref, bias_ref, jnp.where(shift_by_bound, bound, 0.0))

    lane = lax.broadcasted_iota(jnp.int32, (A_Q_TILE, LANES), 1)
    first = lane < HEAD_DIM

    def tile(shifted, sub):
        tile_start = (i * A_STEP_TILES + sub) * A_Q_TILE
        rows = slice(sub * A_Q_TILE, (sub + 1) * A_Q_TILE)
        start = pl.multiple_of(jnp.maximum(tile_start - pad, 0), A_Q_TILE)
        off = 2 * pad - tile_start + start
        for p in range(A_WIDTH // LANES):
            cols = slice(p * LANES, (p + 1) * LANES)
            q = q_ref[0, rows, cols]
            zero = jnp.zeros_like(q)
            qs = jnp.concatenate([jnp.where(first, q, zero), jnp.where(first, zero, q)], axis=0)
            k = k_ref[0, pl.ds(start, A_BAND), cols]
            v = v_ref[0, pl.ds(start, A_BAND), cols]
            s = lax.dot_general(qs, k, (((1,), (1,)), ((), ())), preferred_element_type=F32)
            bias = [bias_ref[2 * p + e, :,
                             pl.ds(pl.multiple_of(off - t * A_STRIP_ROWS, LANES), A_BAND)]
                    for e in range(2) for t in range(A_Q_TILE // A_STRIP_ROWS)]
            s = s + jnp.concatenate(bias, axis=0)
            if not shifted:
                s = s - jnp.max(s, axis=-1, keepdims=True)
            pr = jnp.exp(s)
            l = jnp.sum(pr, axis=-1, keepdims=True)
            o = jnp.dot(pr.astype(BF16), v, preferred_element_type=F32) / l
            o_ref[0, rows, cols] = jnp.where(first, o[:A_Q_TILE], o[A_Q_TILE:]).astype(BF16)

    def step(shifted):
        for sub in range(A_STEP_TILES):
            tile(shifted, sub)

    pl.when(shift_by_bound)(functools.partial(step, True))
    pl.when(jnp.logical_not(shift_by_bound))(functools.partial(step, False))


def _attn_a(bound, q, k, v, f_table):
    b, s, _ = q.shape
    whole = pl.BlockSpec((1, s, A_WIDTH), lambda bi, i: (bi, 0, 0))
    step_rows = A_STEP_TILES * A_Q_TILE
    tile = pl.BlockSpec((1, step_rows, A_WIDTH), lambda bi, i: (bi, i, 0))
    return pl.pallas_call(
        _attn_a_kernel,
        grid=(b, s // step_rows),
        in_specs=[pl.BlockSpec(memory_space=pltpu.SMEM), tile, whole, whole,
                  _resident(f_table.shape)],
        out_specs=tile,
        out_shape=jax.ShapeDtypeStruct(q.shape, BF16),
        scratch_shapes=[pltpu.VMEM((A_HEADS, A_STRIP_ROWS, A_BIAS_W), F32)],
        compiler_params=pltpu.CompilerParams(
            dimension_semantics=("arbitrary", "arbitrary"), vmem_limit_bytes=VMEM_LIMIT),
        name="attn_a",
    )(bound, q, k, v, f_table)


def _clipped_table(rel_table):
    pad = LEFT_CHUNKS * CHUNK
    first, last = rel_table[:1], rel_table[-1:]
    lead = 2 * pad - REL_CLIP
    tail = A_STRIP_ROWS - 1
    mid = A_TABLE_W - lead - rel_table.shape[0] - tail
    f = jnp.concatenate([jnp.repeat(last, lead, axis=0), rel_table[::-1],
                         jnp.repeat(first, mid, axis=0), jnp.repeat(last, tail, axis=0)], axis=0)
    return f.T.astype(F32)


def _attn_b_kernel(bound_ref, q0_ref, q1_ref, k_ref, v_ref, lam_ref, gs_ref, o0_ref, o1_ref,
                   qs_ref, m_ref, l_ref, acc_ref):
    i = pl.program_id(1)
    n_tiles = k_ref.shape[1] // B_Q_TILE
    n_full = n_tiles - 1
    nn = (((1,), (1,)), ((), ()))
    lane = lax.broadcasted_iota(jnp.int32, (B_Q_TILE, LANES), 1)
    first = lane < HEAD_DIM

    def stack_queries():
        for slot, q_ref in enumerate((q0_ref, q1_ref)):
            for h in range(B_HEADS):
                q = q_ref[0, :, h * LANES:(h + 1) * LANES]
                zero = jnp.zeros_like(q)
                qs_ref[slot, h, :B_Q_TILE] = jnp.where(first, q, zero)
                qs_ref[slot, h, B_Q_TILE:] = jnp.where(first, zero, q)

    tiles = (i, n_tiles - 1 - i)

    def scores(slot, h, lo, width, diag_col):
        k = k_ref[0, pl.ds(lo, width), h * LANES:(h + 1) * LANES]
        s = lax.dot_general(qs_ref[slot, h], k, nn, preferred_element_type=F32)
        if diag_col is not None:
            row = lax.broadcasted_iota(jnp.int32, s.shape, 0) & (B_Q_TILE - 1)
            col = lax.broadcasted_iota(jnp.int32, s.shape, 1) - diag_col
            s = jnp.where((col >> 6) <= (row >> 6), s, NEG)
        return [s[:, c * LANES:(c + 1) * LANES] for c in range(width // LANES)]

    def max_block(slot, lo, width, diag_col, assign):
        for h in range(B_HEADS):
            m = functools.reduce(jnp.maximum, scores(slot, h, lo, width, diag_col))
            m_ref[slot, h] = m if assign else jnp.maximum(m_ref[slot, h], m)

    def sum_block(shift, slot, lo, width, diag_col, assign):
        for h in range(B_HEADS):
            m = m_ref[slot, h] if shift is None else shift
            ps = [jnp.exp(s - m) for s in scores(slot, h, lo, width, diag_col)]
            v = v_ref[0, pl.ds(lo, width), h * LANES:(h + 1) * LANES]
            l = functools.reduce(jnp.add, ps)
            pv = jnp.dot(jnp.concatenate(ps, axis=1).astype(BF16), v, preferred_element_type=F32)
            l_ref[slot, h] = l if assign else l_ref[slot, h] + l
            acc_ref[slot, h] = pv if assign else acc_ref[slot, h] + pv

    def sweep(block, unroll):
        for slot in range(2):
            block(slot, pl.multiple_of(tiles[slot] * B_Q_TILE, B_Q_TILE), B_Q_TILE, 0, True)

        def full(t):
            in_slot0 = t < i
            lo = jnp.where(in_slot0, t, t - i) * B_K_TILE
            block(jnp.where(in_slot0, 0, 1), pl.multiple_of(lo, B_K_TILE), B_K_TILE, None, False)

        if unroll:
            for t in range(n_full):
                full(t)
        else:
            def body(t, carry):
                full(t)
                return carry
            lax.fori_loop(0, n_full, body, 0)

    bound = bound_ref[0]
    shift_by_bound = bound <= SHIFT_SAFE

    def finish():
        lq1, lk1, lq2, lk2 = (lam_ref[n:n + 1, :] for n in range(4))
        lam = (jnp.exp(jnp.sum(lq1 * lk1, axis=-1, keepdims=True))
               - jnp.exp(jnp.sum(lq2 * lk2, axis=-1, keepdims=True)) + LAMBDA_INIT)
        for slot, o_ref in enumerate((o0_ref, o1_ref)):
            for h in range(B_HEADS):
                o = acc_ref[slot, h] / jnp.sum(l_ref[slot, h], axis=-1, keepdims=True)
                o = o[:B_Q_TILE] - lam * o[B_Q_TILE:]
                o_ref[0, :, h * LANES:(h + 1) * LANES] = (
                    _rms(o, gs_ref[...]) * (1.0 - LAMBDA_INIT)).astype(BF16)

    @pl.when(shift_by_bound)
    def _():
        stack_queries()
        sweep(functools.partial(sum_block, bound), True)
        finish()

    @pl.when(jnp.logical_not(shift_by_bound))
    def _():
        stack_queries()
        sweep(max_block, False)
        for slot in range(2):
            for h in range(B_HEADS):
                m_ref[slot, h] = jnp.broadcast_to(
                    jnp.max(m_ref[slot, h], axis=-1, keepdims=True), m_ref.shape[2:])
        sweep(functools.partial(sum_block, None), False)
        finish()


def _attn_b(bound, q, k, v, lam_vecs, g_sub):
    b, s, _ = q.shape
    n_tiles = s // B_Q_TILE
    assert n_tiles % 2 == 0 and B_K_TILE == B_Q_TILE
    half = n_tiles // 2
    whole = pl.BlockSpec((1, s, B_WIDTH), lambda bi, i: (bi, 0, 0))
    q_lo = pl.BlockSpec((1, B_Q_TILE, B_WIDTH), lambda bi, i: (bi, i, 0))
    q_hi = pl.BlockSpec((1, B_Q_TILE, B_WIDTH), lambda bi, i: (bi, n_tiles - 1 - i, 0))
    o_hi = pl.BlockSpec((1, B_Q_TILE, B_WIDTH), lambda bi, i: (bi, half - 1 - i, 0))
    stat = pltpu.VMEM((2, B_HEADS, 2 * B_Q_TILE, LANES), F32)
    out = jax.ShapeDtypeStruct((b, s // 2, B_WIDTH), BF16)
    return pl.pallas_call(
        _attn_b_kernel,
        grid=(b, half),
        in_specs=[pl.BlockSpec(memory_space=pltpu.SMEM), q_lo, q_hi, whole, whole,
                  _resident(lam_vecs.shape), _resident(g_sub.shape)],
        out_specs=[q_lo, o_hi],
        out_shape=[out, out],
        scratch_shapes=[pltpu.VMEM((2, B_HEADS, 2 * B_Q_TILE, LANES), BF16), stat, stat, stat],
        compiler_params=pltpu.CompilerParams(
            dimension_semantics=("arbitrary", "arbitrary"), vmem_limit_bytes=VMEM_LIMIT),
        name="attn_b",
    )(bound, q, q, k, v, lam_vecs, g_sub)


def _post_kernel(x_ref, oa_ref, ob_lo_ref, ob_hi_ref, gate_ref, wa_ref, wb_ref, wo_ref,
                 g2_ref, wgu_ref, wdn_ref, gf_ref, o_ref, acc_ref, *, tiles_per_seq):
    ya = jnp.dot(oa_ref[...], wa_ref[...], preferred_element_type=F32)
    in_lo = (pl.program_id(0) % tiles_per_seq) < tiles_per_seq // 2
    ob = jnp.where(in_lo, ob_lo_ref[...], ob_hi_ref[...])
    yb = jnp.dot(ob, wb_ref[...], preferred_element_type=F32)
    y = (jax.nn.sigmoid(gate_ref[:, :D_MODEL]) * ya
         + jax.nn.sigmoid(gate_ref[:, D_MODEL:]) * yb)
    x2 = x_ref[...] + jnp.dot(y.astype(BF16), wo_ref[...], preferred_element_type=F32)
    _swiglu(_rms(x2, g2_ref[...]).astype(BF16), wgu_ref, wdn_ref, acc_ref)
    o_ref[...] = _rms(x2 + 0.5 * acc_ref[...], gf_ref[...])


def _post(x1, oa, ob_lo, ob_hi, gates, wa, wb, wo, g2, wgu, wdn, gf, seq):
    t = x1.shape[0]
    tiles_per_seq = seq // TOKEN_TILE
    half = tiles_per_seq // 2
    assert tiles_per_seq % 2 == 0
    row = lambda w: pl.BlockSpec((TOKEN_TILE, w), lambda i: (i, 0))
    lo = pl.BlockSpec((TOKEN_TILE, B_WIDTH), lambda i: (
        (i // tiles_per_seq) * half + jnp.minimum(i % tiles_per_seq, half - 1), 0))
    hi = pl.BlockSpec((TOKEN_TILE, B_WIDTH), lambda i: (
        (i // tiles_per_seq) * half + jnp.maximum(i % tiles_per_seq - half, 0), 0))
    vec = _resident((1, D_MODEL))
    return pl.pallas_call(
        functools.partial(_post_kernel, tiles_per_seq=tiles_per_seq),
        grid=(t // TOKEN_TILE,),
        in_specs=[row(D_MODEL), row(A_WIDTH), lo, hi, row(2 * D_MODEL),
                  _resident(wa.shape), _resident(wb.shape), _resident(wo.shape),
                  vec, _resident(wgu.shape), _resident(wdn.shape), vec],
        out_specs=row(D_MODEL),
        out_shape=jax.ShapeDtypeStruct(x1.shape, F32),
        scratch_shapes=[pltpu.VMEM((TOKEN_TILE, D_MODEL), F32)],
        compiler_params=pltpu.CompilerParams(
            dimension_semantics=("arbitrary",), vmem_limit_bytes=VMEM_LIMIT),
        name="post",
    )(x1, oa, ob_lo, ob_hi, gates, wa, wb, wo, g2, wgu, wdn, gf)


def _rotary_tables(seq):
    half = ROT_DIM // 2
    pos = np.arange(seq, dtype=np.float64)
    inv = ROPE_THETA ** (-np.arange(0, ROT_DIM, 2, dtype=np.float64) / ROT_DIM)
    ang = pos[:, None] * inv[None, :]
    cos, sin = np.cos(ang), np.sin(ang)
    ones = np.ones((seq, HEAD_DIM - ROT_DIM))
    zeros = np.zeros((seq, HEAD_DIM - half))
    cos_h = np.concatenate([cos, cos, ones], axis=-1)
    sin_lo = np.concatenate([-sin, zeros], axis=-1)
    sin_hi = np.concatenate([np.zeros((seq, half)), sin,
                             np.zeros((seq, HEAD_DIM - ROT_DIM))], axis=-1)
    two = lambda a: jnp.asarray(np.concatenate([a, a], axis=-1), dtype=F32)
    return two(cos_h), two(sin_lo), two(sin_hi)


def kernel(x, g_ffn1, w_ffn1_gu, w_ffn1_down, g_mix, w_in, qn_a, kn_a, rel_bias, qn_b, kn_b,
           lambda_q1, lambda_k1, lambda_q2, lambda_k2, g_subln, w_up_a, w_up_b, w_out, g_ffn2,
           w_ffn2_gu, w_ffn2_down, g_final):
    b, s, d = x.shape
    assert d == D_MODEL and s % TOKEN_TILE == 0 and g_ffn1.shape[0] == 1
    l = 0
    t = b * s
    two = lambda g: jnp.concatenate([g, g], axis=-1)[None, :]
    cos, sin_lo, sin_hi = _rotary_tables(s)

    x1 = _ffn1(x.reshape(t, d), g_ffn1[l][None], w_ffn1_gu[l].astype(BF16),
               w_ffn1_down[l].astype(BF16))
    qa, ka, va, qb, kb, vb, gates = _in_proj(
        x1, g_mix[l][None], w_in[l].astype(BF16), two(qn_a[l]), two(kn_a[l]), two(qn_b[l]),
        two(kn_b[l]), cos, sin_lo, sin_hi, s)
    seq3 = lambda a: a.reshape(b, s, a.shape[-1])
    qk_bound = lambda gq, gk: (1.02 * math.sqrt(HEAD_DIM) * jnp.max(jnp.abs(gq))
                               * jnp.max(jnp.abs(gk))).reshape(1).astype(F32)
    bound_a = qk_bound(qn_a[l], kn_a[l]) + jnp.max(jnp.abs(rel_bias[l])).astype(F32)
    oa = _attn_a(bound_a, seq3(qa), seq3(ka), seq3(va), _clipped_table(rel_bias[l]))
    lam_vecs = jnp.stack([lambda_q1[l], lambda_k1[l], lambda_q2[l], lambda_k2[l]]).astype(F32)
    ob_lo, ob_hi = _attn_b(qk_bound(qn_b[l], kn_b[l]), seq3(qb), seq3(kb), seq3(vb), lam_vecs,
                           g_subln[l][None])
    out = _post(x1, oa.reshape(t, A_WIDTH), ob_lo.reshape(t // 2, B_WIDTH),
                ob_hi.reshape(t // 2, B_WIDTH), gates,
                w_up_a[l].astype(BF16), w_up_b[l].astype(BF16), w_out[l].astype(BF16),
                g_ffn2[l][None], w_ffn2_gu[l].astype(BF16), w_ffn2_down[l].astype(BF16),
                g_final[l][None], s)
    return out.reshape(b, s, d)
```

```python
import functools
import math

import jax
import jax.numpy as jnp
import numpy as np
from jax import lax
from jax.experimental import pallas as pl
from jax.experimental.pallas import tpu as pltpu

F32 = jnp.float32
BF16 = jnp.bfloat16

D_MODEL = 1024
D_FF = 2816
CHUNK = 64
LEFT_CHUNKS = 8
HEAD_DIM = 64
A_HEADS = 8
A_WIDTH = A_HEADS * HEAD_DIM
REL_CLIP = 256
B_HEADS = 4
B_WIDTH = B_HEADS * 2 * HEAD_DIM
ROPE_THETA = 500000.0
ROT_DIM = HEAD_DIM // 4
EPS = 1e-6
NEG = -1e30
IN_WIDTH = 3 * A_WIDTH + 3 * B_WIDTH + 2 * D_MODEL
LAMBDA_INIT = 0.8 - 0.6 * math.exp(-0.3 * 0)

LANES = 128
CAST_ROWS = 16
TOKEN_TILE = 512
FF_TILE = 256
A_Q_TILE = 256
A_BAND = A_Q_TILE + LEFT_CHUNKS * CHUNK
A_STEP_TILES = 4
A_STRIP_ROWS = 128
A_BIAS_W = 2 * LEFT_CHUNKS * CHUNK + A_BAND
A_TABLE_W = 2048
assert A_TABLE_W >= A_BIAS_W + A_STRIP_ROWS - 1 and A_Q_TILE % A_STRIP_ROWS == 0
B_Q_TILE = 256
B_K_TILE = B_Q_TILE
B_STEP_PAIRS = 2
SHIFT_SAFE = 40.0
VMEM_LIMIT = 56 * 1024 * 1024


def _resident(shape):
    return pl.BlockSpec(shape, lambda *_: (0,) * len(shape), pipeline_mode=pl.Buffered(1))


def _rms(x, g):
    ms = jnp.mean(x * x, axis=-1, keepdims=True)
    return x * lax.rsqrt(ms + EPS) * g


def _swiglu(xn, wgu_ref, wdn_ref, acc_ref):
    for c in range(D_FF // FF_TILE):
        lo = c * FF_TILE
        g = jnp.dot(xn, wgu_ref[:, lo:lo + FF_TILE], preferred_element_type=F32)
        u = jnp.dot(xn, wgu_ref[:, D_FF + lo:D_FF + lo + FF_TILE], preferred_element_type=F32)
        a = (g * jax.nn.sigmoid(g) * u).astype(BF16)
        d = jnp.dot(a, wdn_ref[lo:lo + FF_TILE, :], preferred_element_type=F32)
        if c == 0:
            acc_ref[...] = d
        else:
            acc_ref[...] += d


def _cast_view(w, steps):
    cols = w.size // (steps * CAST_ROWS)
    assert cols * steps * CAST_ROWS == w.size and cols % LANES == 0
    return w.reshape(steps * CAST_ROWS, cols)


def _cast_specs(views):
    specs = [pl.BlockSpec((CAST_ROWS, v.shape[1]), lambda i: (i, 0)) for v in views]
    return specs, [jax.ShapeDtypeStruct(v.shape, BF16) for v in views]


def _cast_blocks(src_refs, dst_refs):
    for src, dst in zip(src_refs, dst_refs):
        dst[...] = src[...].astype(BF16)


def _ffn1_kernel(x_ref, g_ref, wgu_ref, wdn_ref, w_next_ref, o_ref, w_next_o_ref, acc_ref):
    x = x_ref[...]
    _swiglu(_rms(x, g_ref[...]).astype(BF16), wgu_ref, wdn_ref, acc_ref)
    o_ref[...] = x + 0.5 * acc_ref[...]
    _cast_blocks([w_next_ref], [w_next_o_ref])


def _ffn1(x, g, wgu, wdn, w_next):
    t = x.shape[0]
    steps = t // TOKEN_TILE
    row = pl.BlockSpec((TOKEN_TILE, D_MODEL), lambda i: (i, 0))
    views = [_cast_view(w_next, steps)]
    cast_specs, cast_shapes = _cast_specs(views)
    x1, w_bf16 = pl.pallas_call(
        _ffn1_kernel,
        grid=(steps,),
        in_specs=[row, _resident((1, D_MODEL)), _resident(wgu.shape), _resident(wdn.shape)]
        + cast_specs,
        out_specs=[row] + cast_specs,
        out_shape=[jax.ShapeDtypeStruct(x.shape, F32)] + cast_shapes,
        scratch_shapes=[pltpu.VMEM((TOKEN_TILE, D_MODEL), F32)],
        compiler_params=pltpu.CompilerParams(
            dimension_semantics=("arbitrary",), vmem_limit_bytes=VMEM_LIMIT),
        name="ffn1",
    )(x, g, wgu, wdn, *views)
    return x1, w_bf16.reshape(w_next.shape)


def _head_norm(x, g2):
    lane = lax.broadcasted_iota(jnp.int32, x.shape, 1)
    lo = lane < HEAD_DIM
    sq = x * x
    s_lo = jnp.sum(jnp.where(lo, sq, 0.0), axis=-1, keepdims=True)
    s_hi = jnp.sum(jnp.where(lo, 0.0, sq), axis=-1, keepdims=True)
    ms = jnp.where(lo, s_lo, s_hi) * (1.0 / HEAD_DIM)
    return x * lax.rsqrt(ms + EPS) * g2


def _rotary(x, cos, sin_lo, sin_hi):
    half = ROT_DIM // 2
    up = pltpu.roll(x, LANES - half, 1)
    dn = pltpu.roll(x, half, 1)
    return x * cos + up * sin_lo + dn * sin_hi


def _in_proj_kernel(x_ref, g_ref, w_ref, qna_ref, kna_ref, qnb_ref, knb_ref,
                    cos_ref, sl_ref, sh_ref, *refs):
    n_cast = (len(refs) - 7) // 2
    w_next_refs, refs = refs[:n_cast], refs[n_cast:]
    qa_ref, ka_ref, va_ref, qb_ref, kb_ref, vb_ref, gate_ref = refs[:7]
    _cast_blocks(w_next_refs, refs[7:])
    h = _rms(x_ref[...], g_ref[...]).astype(BF16)
    scale = 1.0 / math.sqrt(HEAD_DIM)

    def proj(col, width):
        return jnp.dot(h, w_ref[:, col:col + width], preferred_element_type=F32)

    def normed(col, gain_ref, out_ref, rotate, mult):
        y = proj(col, A_WIDTH)
        for p in range(A_WIDTH // LANES):
            t = _head_norm(y[:, p * LANES:(p + 1) * LANES], gain_ref[...])
            if rotate:
                t = _rotary(t, cos_ref[...], sl_ref[...], sh_ref[...])
            if mult != 1.0:
                t = t * mult
            out_ref[:, p * LANES:(p + 1) * LANES] = t.astype(BF16)

    normed(0 * A_WIDTH, qna_ref, qa_ref, False, scale)
    normed(1 * A_WIDTH, kna_ref, ka_ref, False, 1.0)
    va_ref[...] = proj(2 * A_WIDTH, A_WIDTH).astype(BF16)
    normed(3 * A_WIDTH, qnb_ref, qb_ref, True, scale)
    normed(4 * A_WIDTH, knb_ref, kb_ref, True, 1.0)
    vb_ref[...] = proj(5 * A_WIDTH, B_WIDTH).astype(BF16)
    for c in range(2 * D_MODEL // A_WIDTH):
        gate_ref[:, c * A_WIDTH:(c + 1) * A_WIDTH] = proj(6 * A_WIDTH + c * A_WIDTH, A_WIDTH)


def _in_proj(x1, g, w_in, qna, kna, qnb, knb, cos, sin_lo, sin_hi, seq, w_next):
    t = x1.shape[0]
    steps = t // TOKEN_TILE
    tiles_per_seq = seq // TOKEN_TILE
    row = lambda w: pl.BlockSpec((TOKEN_TILE, w), lambda i: (i, 0))
    pos = pl.BlockSpec((TOKEN_TILE, LANES), lambda i: (i % tiles_per_seq, 0))
    gain = _resident((1, LANES))
    half = jax.ShapeDtypeStruct((t, A_WIDTH), BF16)
    views = [_cast_view(w, steps) for w in w_next]
    cast_specs, cast_shapes = _cast_specs(views)
    outs = pl.pallas_call(
        _in_proj_kernel,
        grid=(steps,),
        in_specs=[row(D_MODEL), _resident((1, D_MODEL)), _resident(w_in.shape),
                  gain, gain, gain, gain, pos, pos, pos] + cast_specs,
        out_specs=[row(A_WIDTH)] * 6 + [row(2 * D_MODEL)] + cast_specs,
        out_shape=[half] * 6 + [jax.ShapeDtypeStruct((t, 2 * D_MODEL), F32)] + cast_shapes,
        compiler_params=pltpu.CompilerParams(
            dimension_semantics=("arbitrary",), vmem_limit_bytes=VMEM_LIMIT),
        name="in_proj",
    )(x1, g, w_in, qna, kna, qnb, knb, cos, sin_lo, sin_hi, *views)
    return outs[:7], [o.reshape(w.shape) for o, w in zip(outs[7:], w_next)]


def _fill_band_bias(f_ref, bias_ref, shift):
    pad = LEFT_CHUNKS * CHUNK
    shape = (A_STRIP_ROWS, A_BIAS_W)
    q_chunk = lax.broadcasted_iota(jnp.int32, shape, 0) >> 6
    k_chunk = (lax.broadcasted_iota(jnp.int32, shape, 1) - 2 * pad) >> 6
    valid = jnp.logical_and(k_chunk <= q_chunk, k_chunk >= q_chunk - LEFT_CHUNKS)
    for h in range(A_HEADS):
        row = jnp.broadcast_to(f_ref[h:h + 1, :], (A_STRIP_ROWS, A_TABLE_W))
        skew = pltpu.roll(row, 0, 1, stride=1, stride_axis=0)
        bias_ref[h] = jnp.where(valid, skew[:, :A_BIAS_W] - shift, NEG)


def _attn_a_kernel(bound_ref, q_ref, k_ref, v_ref, f_ref, o_ref, bias_ref):
    i = pl.program_id(1)
    pad = LEFT_CHUNKS * CHUNK
    bound = bound_ref[0]
    shift_by_bound = bound <= SHIFT_SAFE

    @pl.when(jnp.logical_and(pl.program_id(0) == 0, i == 0))
    def _():
        _fill_band_bias(f_ref, bias_ref, jnp.where(shift_by_bound, bound, 0.0))

    lane = lax.broadcasted_iota(jnp.int32, (A_Q_TILE, LANES), 1)
    first = lane < HEAD_DIM

    def tile(shifted, sub):
        tile_start = (i * A_STEP_TILES + sub) * A_Q_TILE
        rows = slice(sub * A_Q_TILE, (sub + 1) * A_Q_TILE)
        start = pl.multiple_of(jnp.maximum(tile_start - pad, 0), A_Q_TILE)
        off = 2 * pad - tile_start + start
        for p in range(A_WIDTH // LANES):
            cols = slice(p * LANES, (p + 1) * LANES)
            q = q_ref[0, rows, cols]
            zero = jnp.zeros_like(q)
            qs = jnp.concatenate([jnp.where(first, q, zero), jnp.where(first, zero, q)], axis=0)
            k = k_ref[0, pl.ds(start, A_BAND), cols]
            v = v_ref[0, pl.ds(start, A_BAND), cols]
            s = lax.dot_general(qs, k, (((1,), (1,)), ((), ())), preferred_element_type=F32)
            bias = [bias_ref[2 * p + e, :,
                             pl.ds(pl.multiple_of(off - t * A_STRIP_ROWS, LANES), A_BAND)]
                    for e in range(2) for t in range(A_Q_TILE // A_STRIP_ROWS)]
            s = s + jnp.concatenate(bias, axis=0)
            if not shifted:
                s = s - jnp.max(s, axis=-1, keepdims=True)
            pr = jnp.exp(s)
            l = jnp.sum(pr, axis=-1, keepdims=True)
            o = jnp.dot(pr.astype(BF16), v, preferred_element_type=F32) / l
            o_ref[0, rows, cols] = jnp.where(first, o[:A_Q_TILE], o[A_Q_TILE:]).astype(BF16)

    def step(shifted):
        for sub in range(A_STEP_TILES):
            tile(shifted, sub)

    pl.when(shift_by_bound)(functools.partial(step, True))
    pl.when(jnp.logical_not(shift_by_bound))(functools.partial(step, False))


def _attn_a(bound, q, k, v, f_table):
    b, s, _ = q.shape
    whole = pl.BlockSpec((1, s, A_WIDTH), lambda bi, i: (bi, 0, 0))
    step_rows = A_STEP_TILES * A_Q_TILE
    tile = pl.BlockSpec((1, step_rows, A_WIDTH), lambda bi, i: (bi, i, 0))
    return pl.pallas_call(
        _attn_a_kernel,
        grid=(b, s // step_rows),
        in_specs=[pl.BlockSpec(memory_space=pltpu.SMEM), tile, whole, whole,
                  _resident(f_table.shape)],
        out_specs=tile,
        out_shape=jax.ShapeDtypeStruct(q.shape, BF16),
        scratch_shapes=[pltpu.VMEM((A_HEADS, A_STRIP_ROWS, A_BIAS_W), F32)],
        compiler_params=pltpu.CompilerParams(
            dimension_semantics=("arbitrary", "arbitrary"), vmem_limit_bytes=VMEM_LIMIT),
        name="attn_a",
    )(bound, q, k, v, f_table)


def _clipped_table(rel_table):
    pad = LEFT_CHUNKS * CHUNK
    first, last = rel_table[:1], rel_table[-1:]
    lead = 2 * pad - REL_CLIP
    tail = A_STRIP_ROWS - 1
    mid = A_TABLE_W - lead - rel_table.shape[0] - tail
    f = jnp.concatenate([jnp.repeat(last, lead, axis=0), rel_table[::-1],
                         jnp.repeat(first, mid, axis=0), jnp.repeat(last, tail, axis=0)], axis=0)
    return f.T.astype(F32)


def _attn_b_kernel(bound_ref, q0_ref, q1_ref, k_ref, v_ref, lam_ref, gs_ref, o0_ref, o1_ref,
                   qs_ref, m_ref, l_ref, acc_ref):
    i = pl.program_id(1)
    n_tiles = k_ref.shape[1] // B_Q_TILE
    n_full = n_tiles - 1
    nn = (((1,), (1,)), ((), ()))
    lane = lax.broadcasted_iota(jnp.int32, (B_Q_TILE, LANES), 1)
    first = lane < HEAD_DIM

    def tile_rows(pair, slot):
        pos = pair if slot == 0 else B_STEP_PAIRS - 1 - pair
        return slice(pos * B_Q_TILE, (pos + 1) * B_Q_TILE)

    def stack_queries(pair):
        for slot, q_ref in enumerate((q0_ref, q1_ref)):
            for h in range(B_HEADS):
                q = q_ref[0, tile_rows(pair, slot), h * LANES:(h + 1) * LANES]
                zero = jnp.zeros_like(q)
                qs_ref[2 * pair + slot, h, :B_Q_TILE] = jnp.where(first, q, zero)
                qs_ref[2 * pair + slot, h, B_Q_TILE:] = jnp.where(first, zero, q)

    def scores(slot, h, lo, width, diag_col):
        k = k_ref[0, pl.ds(lo, width), h * LANES:(h + 1) * LANES]
        s = lax.dot_general(qs_ref[slot, h], k, nn, preferred_element_type=F32)
        if diag_col is not None:
            row = lax.broadcasted_iota(jnp.int32, s.shape, 0) & (B_Q_TILE - 1)
            col = lax.broadcasted_iota(jnp.int32, s.shape, 1) - diag_col
            s = jnp.where((col >> 6) <= (row >> 6), s, NEG)
        return [s[:, c * LANES:(c + 1) * LANES] for c in range(width // LANES)]

    def max_block(slot, lo, width, diag_col, assign):
        for h in range(B_HEADS):
            m = functools.reduce(jnp.maximum, scores(slot, h, lo, width, diag_col))
            m_ref[slot, h] = m if assign else jnp.maximum(m_ref[slot, h], m)

    def sum_block(shift, slot, lo, width, diag_col, assign):
        for h in range(B_HEADS):
            m = m_ref[slot, h] if shift is None else shift
            ps = [jnp.exp(s - m) for s in scores(slot, h, lo, width, diag_col)]
            v = v_ref[0, pl.ds(lo, width), h * LANES:(h + 1) * LANES]
            l = functools.reduce(jnp.add, ps)
            pv = jnp.dot(jnp.concatenate(ps, axis=1).astype(BF16), v, preferred_element_type=F32)
            l_ref[slot, h] = l if assign else l_ref[slot, h] + l
            acc_ref[slot, h] = pv if assign else acc_ref[slot, h] + pv

    def sweep(block, unroll, pair):
        a = i * B_STEP_PAIRS + pair
        for slot, tile in enumerate((a, n_tiles - 1 - a)):
            block(2 * pair + slot, pl.multiple_of(tile * B_Q_TILE, B_Q_TILE), B_Q_TILE, 0, True)

        def full(t):
            in_slot0 = t < a
            lo = jnp.where(in_slot0, t, t - a) * B_K_TILE
            block(2 * pair + jnp.where(in_slot0, 0, 1), pl.multiple_of(lo, B_K_TILE), B_K_TILE,
                  None, False)

        if unroll:
            for t in range(n_full):
                full(t)
        else:
            def body(t, carry):
                full(t)
                return carry
            lax.fori_loop(0, n_full, body, 0)

    bound = bound_ref[0]
    shift_by_bound = bound <= SHIFT_SAFE

    def finish(pair):
        lq1, lk1, lq2, lk2 = (lam_ref[n:n + 1, :] for n in range(4))
        lam = (jnp.exp(jnp.sum(lq1 * lk1, axis=-1, keepdims=True))
               - jnp.exp(jnp.sum(lq2 * lk2, axis=-1, keepdims=True)) + LAMBDA_INIT)
        for slot, o_ref in enumerate((o0_ref, o1_ref)):
            idx = 2 * pair + slot
            for h in range(B_HEADS):
                o = acc_ref[idx, h] / jnp.sum(l_ref[idx, h], axis=-1, keepdims=True)
                o = o[:B_Q_TILE] - lam * o[B_Q_TILE:]
                o_ref[0, tile_rows(pair, slot), h * LANES:(h + 1) * LANES] = (
                    _rms(o, gs_ref[...]) * (1.0 - LAMBDA_INIT)).astype(BF16)

    @pl.when(shift_by_bound)
    def _():
        for pair in range(B_STEP_PAIRS):
            stack_queries(pair)
            sweep(functools.partial(sum_block, bound), True, pair)
            finish(pair)

    @pl.when(jnp.logical_not(shift_by_bound))
    def _():
        for pair in range(B_STEP_PAIRS):
            stack_queries(pair)
            sweep(max_block, False, pair)
            for idx in (2 * pair, 2 * pair + 1):
                for h in range(B_HEADS):
                    m_ref[idx, h] = jnp.broadcast_to(
                        jnp.max(m_ref[idx, h], axis=-1, keepdims=True), m_ref.shape[2:])
            sweep(functools.partial(sum_block, None), False, pair)
            finish(pair)


def _attn_b(bound, q, k, v, lam_vecs, g_sub):
    b, s, _ = q.shape
    n_tiles = s // B_Q_TILE
    assert n_tiles % (2 * B_STEP_PAIRS) == 0 and B_K_TILE == B_Q_TILE
    steps = n_tiles // (2 * B_STEP_PAIRS)
    rows = B_STEP_PAIRS * B_Q_TILE
    whole = pl.BlockSpec((1, s, B_WIDTH), lambda bi, i: (bi, 0, 0))
    q_lo = pl.BlockSpec((1, rows, B_WIDTH), lambda bi, i: (bi, i, 0))
    q_hi = pl.BlockSpec((1, rows, B_WIDTH), lambda bi, i: (bi, 2 * steps - 1 - i, 0))
    o_hi = pl.BlockSpec((1, rows, B_WIDTH), lambda bi, i: (bi, steps - 1 - i, 0))
    stat = pltpu.VMEM((2 * B_STEP_PAIRS, B_HEADS, 2 * B_Q_TILE, LANES), F32)
    out = jax.ShapeDtypeStruct((b, s // 2, B_WIDTH), BF16)
    return pl.pallas_call(
        _attn_b_kernel,
        grid=(b, steps),
        in_specs=[pl.BlockSpec(memory_space=pltpu.SMEM), q_lo, q_hi, whole, whole,
                  _resident(lam_vecs.shape), _resident(g_sub.shape)],
        out_specs=[q_lo, o_hi],
        out_shape=[out, out],
        scratch_shapes=[pltpu.VMEM((2 * B_STEP_PAIRS, B_HEADS, 2 * B_Q_TILE, LANES), BF16),
                        stat, stat, stat],
        compiler_params=pltpu.CompilerParams(
            dimension_semantics=("arbitrary", "arbitrary"), vmem_limit_bytes=VMEM_LIMIT),
        name="attn_b",
    )(bound, q, q, k, v, lam_vecs, g_sub)


def _post_kernel(x_ref, oa_ref, ob_lo_ref, ob_hi_ref, gate_ref, wa_ref, wb_ref, wo_ref,
                 g2_ref, wgu_ref, wdn_ref, gf_ref, o_ref, acc_ref, *, tiles_per_seq):
    ya = jnp.dot(oa_ref[...], wa_ref[...], preferred_element_type=F32)
    in_lo = (pl.program_id(0) % tiles_per_seq) < tiles_per_seq // 2
    ob = jnp.where(in_lo, ob_lo_ref[...], ob_hi_ref[...])
    yb = jnp.dot(ob, wb_ref[...], preferred_element_type=F32)
    y = (jax.nn.sigmoid(gate_ref[:, :D_MODEL]) * ya
         + jax.nn.sigmoid(gate_ref[:, D_MODEL:]) * yb)
    x2 = x_ref[...] + jnp.dot(y.astype(BF16), wo_ref[...], preferred_element_type=F32)
    _swiglu(_rms(x2, g2_ref[...]).astype(BF16), wgu_ref, wdn_ref, acc_ref)
    o_ref[...] = _rms(x2 + 0.5 * acc_ref[...], gf_ref[...])


def _post(x1, oa, ob_lo, ob_hi, gates, wa, wb, wo, g2, wgu, wdn, gf, seq):
    t = x1.shape[0]
    tiles_per_seq = seq // TOKEN_TILE
    half = tiles_per_seq // 2
    assert tiles_per_seq % 2 == 0
    row = lambda w: pl.BlockSpec((TOKEN_TILE, w), lambda i: (i, 0))
    lo = pl.BlockSpec((TOKEN_TILE, B_WIDTH), lambda i: (
        (i // tiles_per_seq) * half + jnp.minimum(i % tiles_per_seq, half - 1), 0))
    hi = pl.BlockSpec((TOKEN_TILE, B_WIDTH), lambda i: (
        (i // tiles_per_seq) * half + jnp.maximum(i % tiles_per_seq - half, 0), 0))
    vec = _resident((1, D_MODEL))
    return pl.pallas_call(
        functools.partial(_post_kernel, tiles_per_seq=tiles_per_seq),
        grid=(t // TOKEN_TILE,),
        in_specs=[row(D_MODEL), row(A_WIDTH), lo, hi, row(2 * D_MODEL),
                  _resident(wa.shape), _resident(wb.shape), _resident(wo.shape),
                  vec, _resident(wgu.shape), _resident(wdn.shape), vec],
        out_specs=row(D_MODEL),
        out_shape=jax.ShapeDtypeStruct(x1.shape, F32),
        scratch_shapes=[pltpu.VMEM((TOKEN_TILE, D_MODEL), F32)],
        compiler_params=pltpu.CompilerParams(
            dimension_semantics=("arbitrary",), vmem_limit_bytes=VMEM_LIMIT),
        name="post",
    )(x1, oa, ob_lo, ob_hi, gates, wa, wb, wo, g2, wgu, wdn, gf)


def _rotary_tables(seq):
    half = ROT_DIM // 2
    pos = np.arange(seq, dtype=np.float64)
    inv = ROPE_THETA ** (-np.arange(0, ROT_DIM, 2, dtype=np.float64) / ROT_DIM)
    ang = pos[:, None] * inv[None, :]
    cos, sin = np.cos(ang), np.sin(ang)
    ones = np.ones((seq, HEAD_DIM - ROT_DIM))
    zeros = np.zeros((seq, HEAD_DIM - half))
    cos_h = np.concatenate([cos, cos, ones], axis=-1)
    sin_lo = np.concatenate([-sin, zeros], axis=-1)
    sin_hi = np.concatenate([np.zeros((seq, half)), sin,
                             np.zeros((seq, HEAD_DIM - ROT_DIM))], axis=-1)
    two = lambda a: jnp.asarray(np.concatenate([a, a], axis=-1), dtype=F32)
    return two(cos_h), two(sin_lo), two(sin_hi)


def kernel(x, g_ffn1, w_ffn1_gu, w_ffn1_down, g_mix, w_in, qn_a, kn_a, rel_bias, qn_b, kn_b,
           lambda_q1, lambda_k1, lambda_q2, lambda_k2, g_subln, w_up_a, w_up_b, w_out, g_ffn2,
           w_ffn2_gu, w_ffn2_down, g_final):
    b, s, d = x.shape
    assert d == D_MODEL and s % TOKEN_TILE == 0 and g_ffn1.shape[0] == 1
    l = 0
    t = b * s
    two = lambda g: jnp.concatenate([g, g], axis=-1)[None, :]
    cos, sin_lo, sin_hi = _rotary_tables(s)

    x1, w_in_bf16 = _ffn1(x.reshape(t, d), g_ffn1[l][None], w_ffn1_gu[l].astype(BF16),
                          w_ffn1_down[l].astype(BF16), w_in[l])
    (qa, ka, va, qb, kb, vb, gates), post_w = _in_proj(
        x1, g_mix[l][None], w_in_bf16, two(qn_a[l]), two(kn_a[l]), two(qn_b[l]),
        two(kn_b[l]), cos, sin_lo, sin_hi, s,
        [w_up_a[l], w_up_b[l], w_out[l], w_ffn2_gu[l], w_ffn2_down[l]])
    wa, wb, wo, wgu2, wdn2 = post_w
    seq3 = lambda a: a.reshape(b, s, a.shape[-1])
    qk_bound = lambda gq, gk: (1.02 * math.sqrt(HEAD_DIM) * jnp.max(jnp.abs(gq))
                               * jnp.max(jnp.abs(gk))).reshape(1).astype(F32)
    bound_a = qk_bound(qn_a[l], kn_a[l]) + jnp.max(jnp.abs(rel_bias[l])).astype(F32)
    oa = _attn_a(bound_a, seq3(qa), seq3(ka), seq3(va), _clipped_table(rel_bias[l]))
    lam_vecs = jnp.stack([lambda_q1[l], lambda_k1[l], lambda_q2[l], lambda_k2[l]]).astype(F32)
    ob_lo, ob_hi = _attn_b(qk_bound(qn_b[l], kn_b[l]), seq3(qb), seq3(kb), seq3(vb), lam_vecs,
                           g_subln[l][None])
    out = _post(x1, oa.reshape(t, A_WIDTH), ob_lo.reshape(t // 2, B_WIDTH),
                ob_hi.reshape(t // 2, B_WIDTH), gates, wa, wb, wo,
                g_ffn2[l][None], wgu2, wdn2, g_final[l][None], s)
    return out.reshape(b, s, d)
```

```python
import functools
import math

import jax
import jax.numpy as jnp
import numpy as np
from jax import lax
from jax.experimental import pallas as pl
from jax.experimental.pallas import tpu as pltpu

F32 = jnp.float32
BF16 = jnp.bfloat16

D_MODEL = 1024
D_FF = 2816
CHUNK = 64
LEFT_CHUNKS = 8
HEAD_DIM = 64
A_HEADS = 8
A_WIDTH = A_HEADS * HEAD_DIM
REL_CLIP = 256
B_HEADS = 4
B_WIDTH = B_HEADS * 2 * HEAD_DIM
ROPE_THETA = 500000.0
ROT_DIM = HEAD_DIM // 4
EPS = 1e-6
NEG = -1e30
IN_WIDTH = 3 * A_WIDTH + 3 * B_WIDTH + 2 * D_MODEL
LAMBDA_INIT = 0.8 - 0.6 * math.exp(-0.3 * 0)

LANES = 128
CAST_ROWS = 16
TOKEN_TILE = 512
FF_TILE = 256
A_Q_TILE = 256
A_BAND = A_Q_TILE + LEFT_CHUNKS * CHUNK
A_STEP_TILES = 4
A_STRIP_ROWS = 128
A_BIAS_W = 2 * LEFT_CHUNKS * CHUNK + A_BAND
A_TABLE_W = 2048
assert A_TABLE_W >= A_BIAS_W + A_STRIP_ROWS - 1 and A_Q_TILE % A_STRIP_ROWS == 0
B_Q_TILE = 256
B_K_TILE = B_Q_TILE
B_STEP_PAIRS = 1
SHIFT_SAFE = 40.0
VMEM_LIMIT = 56 * 1024 * 1024


def _resident(shape):
    return pl.BlockSpec(shape, lambda *_: (0,) * len(shape), pipeline_mode=pl.Buffered(1))


def _rms(x, g):
    ms = jnp.mean(x * x, axis=-1, keepdims=True)
    return x * lax.rsqrt(ms + EPS) * g


def _swiglu(xn, wgu_ref, wdn_ref, acc_ref):
    for c in range(D_FF // FF_TILE):
        lo = c * FF_TILE
        g = jnp.dot(xn, wgu_ref[:, lo:lo + FF_TILE], preferred_element_type=F32)
        u = jnp.dot(xn, wgu_ref[:, D_FF + lo:D_FF + lo + FF_TILE], preferred_element_type=F32)
        a = (g * jax.nn.sigmoid(g) * u).astype(BF16)
        d = jnp.dot(a, wdn_ref[lo:lo + FF_TILE, :], preferred_element_type=F32)
        if c == 0:
            acc_ref[...] = d
        else:
            acc_ref[...] += d


def _cast_blocks_of(w, steps):
    rows = w.shape[0]
    n = max(d for d in range(1, steps + 1) if rows % (d * CAST_ROWS) == 0)
    return n


def _cast_specs(weights, steps):
    specs, counts = [], []
    for w in weights:
        n = _cast_blocks_of(w, steps)
        specs.append(pl.BlockSpec((w.shape[0] // n, w.shape[1]),
                                  lambda i, n=n: (jnp.minimum(i, n - 1), 0)))
        counts.append(n)
    return specs, [jax.ShapeDtypeStruct(w.shape, BF16) for w in weights], counts


def _cast_blocks(src_refs, dst_refs, counts):
    for src, dst, n in zip(src_refs, dst_refs, counts):
        @pl.when(pl.program_id(0) < n)
        def _():
            dst[...] = src[...].astype(BF16)


def _ffn1_kernel(x_ref, g_ref, wgu_ref, wdn_ref, w_next_ref, o_ref, w_next_o_ref, acc_ref, *,
                 cast_counts):
    x = x_ref[...]
    _swiglu(_rms(x, g_ref[...]).astype(BF16), wgu_ref, wdn_ref, acc_ref)
    o_ref[...] = x + 0.5 * acc_ref[...]
    _cast_blocks([w_next_ref], [w_next_o_ref], cast_counts)


def _ffn1(x, g, wgu, wdn, w_next):
    t = x.shape[0]
    steps = t // TOKEN_TILE
    row = pl.BlockSpec((TOKEN_TILE, D_MODEL), lambda i: (i, 0))
    cast_specs, cast_shapes, cast_counts = _cast_specs([w_next], steps)
    return pl.pallas_call(
        functools.partial(_ffn1_kernel, cast_counts=cast_counts),
        grid=(steps,),
        in_specs=[row, _resident((1, D_MODEL)), _resident(wgu.shape), _resident(wdn.shape)]
        + cast_specs,
        out_specs=[row] + cast_specs,
        out_shape=[jax.ShapeDtypeStruct(x.shape, F32)] + cast_shapes,
        scratch_shapes=[pltpu.VMEM((TOKEN_TILE, D_MODEL), F32)],
        compiler_params=pltpu.CompilerParams(
            dimension_semantics=("arbitrary",), vmem_limit_bytes=VMEM_LIMIT),
        name="ffn1",
    )(x, g, wgu, wdn, w_next)


def _head_norm(x, g2):
    lane = lax.broadcasted_iota(jnp.int32, x.shape, 1)
    lo = lane < HEAD_DIM
    sq = x * x
    s_lo = jnp.sum(jnp.where(lo, sq, 0.0), axis=-1, keepdims=True)
    s_hi = jnp.sum(jnp.where(lo, 0.0, sq), axis=-1, keepdims=True)
    ms = jnp.where(lo, s_lo, s_hi) * (1.0 / HEAD_DIM)
    return x * lax.rsqrt(ms + EPS) * g2


def _rotary(x, cos, sin_lo, sin_hi):
    half = ROT_DIM // 2
    up = pltpu.roll(x, LANES - half, 1)
    dn = pltpu.roll(x, half, 1)
    return x * cos + up * sin_lo + dn * sin_hi


def _in_proj_kernel(x_ref, g_ref, w_ref, qna_ref, kna_ref, qnb_ref, knb_ref,
                    cos_ref, sl_ref, sh_ref, *refs, cast_counts):
    n_cast = len(cast_counts)
    w_next_refs, refs = refs[:n_cast], refs[n_cast:]
    qa_ref, ka_ref, va_ref, qb_ref, kb_ref, vb_ref, gate_ref = refs[:7]
    _cast_blocks(w_next_refs, refs[7:], cast_counts)
    h = _rms(x_ref[...], g_ref[...]).astype(BF16)
    scale = 1.0 / math.sqrt(HEAD_DIM)

    def proj(col, width):
        return jnp.dot(h, w_ref[:, col:col + width], preferred_element_type=F32)

    def normed(col, gain_ref, out_ref, rotate, mult):
        y = proj(col, A_WIDTH)
        for p in range(A_WIDTH // LANES):
            t = _head_norm(y[:, p * LANES:(p + 1) * LANES], gain_ref[...])
            if rotate:
                t = _rotary(t, cos_ref[...], sl_ref[...], sh_ref[...])
            if mult != 1.0:
                t = t * mult
            out_ref[:, p * LANES:(p + 1) * LANES] = t.astype(BF16)

    normed(0 * A_WIDTH, qna_ref, qa_ref, False, scale)
    normed(1 * A_WIDTH, kna_ref, ka_ref, False, 1.0)
    va_ref[...] = proj(2 * A_WIDTH, A_WIDTH).astype(BF16)
    normed(3 * A_WIDTH, qnb_ref, qb_ref, True, scale)
    normed(4 * A_WIDTH, knb_ref, kb_ref, True, 1.0)
    vb_ref[...] = proj(5 * A_WIDTH, B_WIDTH).astype(BF16)
    for c in range(2 * D_MODEL // A_WIDTH):
        gate_ref[:, c * A_WIDTH:(c + 1) * A_WIDTH] = proj(6 * A_WIDTH + c * A_WIDTH, A_WIDTH)


def _in_proj(x1, g, w_in, qna, kna, qnb, knb, cos, sin_lo, sin_hi, seq, w_next):
    t = x1.shape[0]
    steps = t // TOKEN_TILE
    tiles_per_seq = seq // TOKEN_TILE
    row = lambda w: pl.BlockSpec((TOKEN_TILE, w), lambda i: (i, 0))
    pos = pl.BlockSpec((TOKEN_TILE, LANES), lambda i: (i % tiles_per_seq, 0))
    gain = _resident((1, LANES))
    half = jax.ShapeDtypeStruct((t, A_WIDTH), BF16)
    cast_specs, cast_shapes, cast_counts = _cast_specs(w_next, steps)
    outs = pl.pallas_call(
        functools.partial(_in_proj_kernel, cast_counts=cast_counts),
        grid=(steps,),
        in_specs=[row(D_MODEL), _resident((1, D_MODEL)), _resident(w_in.shape),
                  gain, gain, gain, gain, pos, pos, pos] + cast_specs,
        out_specs=[row(A_WIDTH)] * 6 + [row(2 * D_MODEL)] + cast_specs,
        out_shape=[half] * 6 + [jax.ShapeDtypeStruct((t, 2 * D_MODEL), F32)] + cast_shapes,
        compiler_params=pltpu.CompilerParams(
            dimension_semantics=("arbitrary",), vmem_limit_bytes=VMEM_LIMIT),
        name="in_proj",
    )(x1, g, w_in, qna, kna, qnb, knb, cos, sin_lo, sin_hi, *w_next)
    return outs[:7], outs[7:]


def _fill_band_bias(f_ref, bias_ref, shift):
    pad = LEFT_CHUNKS * CHUNK
    shape = (A_STRIP_ROWS, A_BIAS_W)
    q_chunk = lax.broadcasted_iota(jnp.int32, shape, 0) >> 6
    k_chunk = (lax.broadcasted_iota(jnp.int32, shape, 1) - 2 * pad) >> 6
    valid = jnp.logical_and(k_chunk <= q_chunk, k_chunk >= q_chunk - LEFT_CHUNKS)
    for h in range(A_HEADS):
        row = jnp.broadcast_to(f_ref[h:h + 1, :], (A_STRIP_ROWS, A_TABLE_W))
        skew = pltpu.roll(row, 0, 1, stride=1, stride_axis=0)
        bias_ref[h] = jnp.where(valid, skew[:, :A_BIAS_W] - shift, NEG)


def _attn_a_kernel(bound_ref, q_ref, k_ref, v_ref, f_ref, o_ref, bias_ref):
    i = pl.program_id(1)
    pad = LEFT_CHUNKS * CHUNK
    bound = bound_ref[0]
    shift_by_bound = bound <= SHIFT_SAFE

    @pl.when(jnp.logical_and(pl.program_id(0) == 0, i == 0))
    def _():
        _fill_band_bias(f_ref, bias_ref, jnp.where(shift_by_bound, bound, 0.0))

    lane = lax.broadcasted_iota(jnp.int32, (A_Q_TILE, LANES), 1)
    first = lane < HEAD_DIM

    def tile(shifted, sub):
        tile_start = (i * A_STEP_TILES + sub) * A_Q_TILE
        rows = slice(sub * A_Q_TILE, (sub + 1) * A_Q_TILE)
        start = pl.multiple_of(jnp.maximum(tile_start - pad, 0), A_Q_TILE)
        off = 2 * pad - tile_start + start
        for p in range(A_WIDTH // LANES):
            cols = slice(p * LANES, (p + 1) * LANES)
            q = q_ref[0, rows, cols]
            zero = jnp.zeros_like(q)
            qs = jnp.concatenate([jnp.where(first, q, zero), jnp.where(first, zero, q)], axis=0)
            k = k_ref[0, pl.ds(start, A_BAND), cols]
            v = v_ref[0, pl.ds(start, A_BAND), cols]
            s = lax.dot_general(qs, k, (((1,), (1,)), ((), ())), preferred_element_type=F32)
            bias = [bias_ref[2 * p + e, :,
                             pl.ds(pl.multiple_of(off - t * A_STRIP_ROWS, LANES), A_BAND)]
                    for e in range(2) for t in range(A_Q_TILE // A_STRIP_ROWS)]
            s = s + jnp.concatenate(bias, axis=0)
            if not shifted:
                s = s - jnp.max(s, axis=-1, keepdims=True)
            pr = jnp.exp(s)
            l = jnp.sum(pr, axis=-1, keepdims=True)
            o = jnp.dot(pr.astype(BF16), v, preferred_element_type=F32) / l
            o_ref[0, rows, cols] = jnp.where(first, o[:A_Q_TILE], o[A_Q_TILE:]).astype(BF16)

    def step(shifted):
        for sub in range(A_STEP_TILES):
            tile(shifted, sub)

    pl.when(shift_by_bound)(functools.partial(step, True))
    pl.when(jnp.logical_not(shift_by_bound))(functools.partial(step, False))


def _attn_a(bound, q, k, v, f_table):
    b, s, _ = q.shape
    whole = pl.BlockSpec((1, s, A_WIDTH), lambda bi, i: (bi, 0, 0))
    step_rows = A_STEP_TILES * A_Q_TILE
    tile = pl.BlockSpec((1, step_rows, A_WIDTH), lambda bi, i: (bi, i, 0))
    return pl.pallas_call(
        _attn_a_kernel,
        grid=(b, s // step_rows),
        in_specs=[pl.BlockSpec(memory_space=pltpu.SMEM), tile, whole, whole,
                  _resident(f_table.shape)],
        out_specs=tile,
        out_shape=jax.ShapeDtypeStruct(q.shape, BF16),
        scratch_shapes=[pltpu.VMEM((A_HEADS, A_STRIP_ROWS, A_BIAS_W), F32)],
        compiler_params=pltpu.CompilerParams(
            dimension_semantics=("arbitrary", "arbitrary"), vmem_limit_bytes=VMEM_LIMIT),
        name="attn_a",
    )(bound, q, k, v, f_table)


def _clipped_table(rel_table):
    pad = LEFT_CHUNKS * CHUNK
    first, last = rel_table[:1], rel_table[-1:]
    lead = 2 * pad - REL_CLIP
    tail = A_STRIP_ROWS - 1
    mid = A_TABLE_W - lead - rel_table.shape[0] - tail
    f = jnp.concatenate([jnp.repeat(last, lead, axis=0), rel_table[::-1],
                         jnp.repeat(first, mid, axis=0), jnp.repeat(last, tail, axis=0)], axis=0)
    return f.T.astype(F32)


def _attn_b_kernel(bound_ref, q0_ref, q1_ref, k_ref, v_ref, lam_ref, gs_ref, o0_ref, o1_ref,
                   qs_ref, m_ref, l_ref, acc_ref):
    i = pl.program_id(1)
    n_tiles = k_ref.shape[1] // B_Q_TILE
    n_full = n_tiles - 1
    nn = (((1,), (1,)), ((), ()))
    lane = lax.broadcasted_iota(jnp.int32, (B_Q_TILE, LANES), 1)
    first = lane < HEAD_DIM

    def tile_rows(pair, slot):
        pos = pair if slot == 0 else B_STEP_PAIRS - 1 - pair
        return slice(pos * B_Q_TILE, (pos + 1) * B_Q_TILE)

    def stack_queries(pair):
        for slot, q_ref in enumerate((q0_ref, q1_ref)):
            for h in range(B_HEADS):
                q = q_ref[0, tile_rows(pair, slot), h * LANES:(h + 1) * LANES]
                zero = jnp.zeros_like(q)
                qs_ref[2 * pair + slot, h, :B_Q_TILE] = jnp.where(first, q, zero)
                qs_ref[2 * pair + slot, h, B_Q_TILE:] = jnp.where(first, zero, q)

    def scores(slot, h, lo, width, diag_col):
        k = k_ref[0, pl.ds(lo, width), h * LANES:(h + 1) * LANES]
        s = lax.dot_general(qs_ref[slot, h], k, nn, preferred_element_type=F32)
        if diag_col is not None:
            row = lax.broadcasted_iota(jnp.int32, s.shape, 0) & (B_Q_TILE - 1)
            col = lax.broadcasted_iota(jnp.int32, s.shape, 1) - diag_col
            s = jnp.where((col >> 6) <= (row >> 6), s, NEG)
        return [s[:, c * LANES:(c + 1) * LANES] for c in range(width // LANES)]

    def max_block(slot, lo, width, diag_col, assign):
        for h in range(B_HEADS):
            m = functools.reduce(jnp.maximum, scores(slot, h, lo, width, diag_col))
            m_ref[slot, h] = m if assign else jnp.maximum(m_ref[slot, h], m)

    def sum_block(shift, slot, lo, width, diag_col, assign):
        for h in range(B_HEADS):
            m = m_ref[slot, h] if shift is None else shift
            ps = [jnp.exp(s - m) for s in scores(slot, h, lo, width, diag_col)]
            v = v_ref[0, pl.ds(lo, width), h * LANES:(h + 1) * LANES]
            l = functools.reduce(jnp.add, ps)
            pv = jnp.dot(jnp.concatenate(ps, axis=1).astype(BF16), v, preferred_element_type=F32)
            l_ref[slot, h] = l if assign else l_ref[slot, h] + l
            acc_ref[slot, h] = pv if assign else acc_ref[slot, h] + pv

    def sweep(block, unroll, pair):
        a = i * B_STEP_PAIRS + pair
        for slot, tile in enumerate((a, n_tiles - 1 - a)):
            block(2 * pair + slot, pl.multiple_of(tile * B_Q_TILE, B_Q_TILE), B_Q_TILE, 0, True)

        def full(t):
            in_slot0 = t < a
            lo = jnp.where(in_slot0, t, t - a) * B_K_TILE
            block(2 * pair + jnp.where(in_slot0, 0, 1), pl.multiple_of(lo, B_K_TILE), B_K_TILE,
                  None, False)

        if unroll:
            for t in range(n_full):
                full(t)
        else:
            def body(t, carry):
                full(t)
                return carry
            lax.fori_loop(0, n_full, body, 0)

    bound = bound_ref[0]
    shift_by_bound = bound <= SHIFT_SAFE

    def finish(pair):
        lq1, lk1, lq2, lk2 = (lam_ref[n:n + 1, :] for n in range(4))
        lam = (jnp.exp(jnp.sum(lq1 * lk1, axis=-1, keepdims=True))
               - jnp.exp(jnp.sum(lq2 * lk2, axis=-1, keepdims=True)) + LAMBDA_INIT)
        for slot, o_ref in enumerate((o0_ref, o1_ref)):
            idx = 2 * pair + slot
            for h in range(B_HEADS):
                o = acc_ref[idx, h] / jnp.sum(l_ref[idx, h], axis=-1, keepdims=True)
                o = o[:B_Q_TILE] - lam * o[B_Q_TILE:]
                o_ref[0, tile_rows(pair, slot), h * LANES:(h + 1) * LANES] = (
                    _rms(o, gs_ref[...]) * (1.0 - LAMBDA_INIT)).astype(BF16)

    @pl.when(shift_by_bound)
    def _():
        for pair in range(B_STEP_PAIRS):
            stack_queries(pair)
            sweep(functools.partial(sum_block, bound), True, pair)
            finish(pair)

    @pl.when(jnp.logical_not(shift_by_bound))
    def _():
        for pair in range(B_STEP_PAIRS):
            stack_queries(pair)
            sweep(max_block, False, pair)
            for idx in (2 * pair, 2 * pair + 1):
                for h in range(B_HEADS):
                    m_ref[idx, h] = jnp.broadcast_to(
                        jnp.max(m_ref[idx, h], axis=-1, keepdims=True), m_ref.shape[2:])
            sweep(functools.partial(sum_block, None), False, pair)
            finish(pair)


def _attn_b(bound, q, k, v, lam_vecs, g_sub):
    b, s, _ = q.shape
    n_tiles = s // B_Q_TILE
    assert n_tiles % (2 * B_STEP_PAIRS) == 0 and B_K_TILE == B_Q_TILE
    steps = n_tiles // (2 * B_STEP_PAIRS)
    rows = B_STEP_PAIRS * B_Q_TILE
    whole = pl.BlockSpec((1, s, B_WIDTH), lambda bi, i: (bi, 0, 0))
    q_lo = pl.BlockSpec((1, rows, B_WIDTH), lambda bi, i: (bi, i, 0))
    q_hi = pl.BlockSpec((1, rows, B_WIDTH), lambda bi, i: (bi, 2 * steps - 1 - i, 0))
    o_hi = pl.BlockSpec((1, rows, B_WIDTH), lambda bi, i: (bi, steps - 1 - i, 0))
    stat = pltpu.VMEM((2 * B_STEP_PAIRS, B_HEADS, 2 * B_Q_TILE, LANES), F32)
    out = jax.ShapeDtypeStruct((b, s // 2, B_WIDTH), BF16)
    return pl.pallas_call(
        _attn_b_kernel,
        grid=(b, steps),
        in_specs=[pl.BlockSpec(memory_space=pltpu.SMEM), q_lo, q_hi, whole, whole,
                  _resident(lam_vecs.shape), _resident(g_sub.shape)],
        out_specs=[q_lo, o_hi],
        out_shape=[out, out],
        scratch_shapes=[pltpu.VMEM((2 * B_STEP_PAIRS, B_HEADS, 2 * B_Q_TILE, LANES), BF16),
                        stat, stat, stat],
        compiler_params=pltpu.CompilerParams(
            dimension_semantics=("arbitrary", "arbitrary"), vmem_limit_bytes=VMEM_LIMIT),
        name="attn_b",
    )(bound, q, q, k, v, lam_vecs, g_sub)


def _post_kernel(x_ref, oa_ref, ob_lo_ref, ob_hi_ref, gate_ref, wa_ref, wb_ref, wo_ref,
                 g2_ref, wgu_ref, wdn_ref, gf_ref, o_ref, acc_ref, *, tiles_per_seq):
    ya = jnp.dot(oa_ref[...], wa_ref[...], preferred_element_type=F32)
    in_lo = (pl.program_id(0) % tiles_per_seq) < tiles_per_seq // 2
    ob = jnp.where(in_lo, ob_lo_ref[...], ob_hi_ref[...])
    yb = jnp.dot(ob, wb_ref[...], preferred_element_type=F32)
    y = (jax.nn.sigmoid(gate_ref[:, :D_MODEL]) * ya
         + jax.nn.sigmoid(gate_ref[:, D_MODEL:]) * yb)
    x2 = x_ref[...] + jnp.dot(y.astype(BF16), wo_ref[...], preferred_element_type=F32)
    _swiglu(_rms(x2, g2_ref[...]).astype(BF16), wgu_ref, wdn_ref, acc_ref)
    o_ref[...] = _rms(x2 + 0.5 * acc_ref[...], gf_ref[...])


def _post(x1, oa, ob_lo, ob_hi, gates, wa, wb, wo, g2, wgu, wdn, gf, seq):
    t = x1.shape[0]
    tiles_per_seq = seq // TOKEN_TILE
    half = tiles_per_seq // 2
    assert tiles_per_seq % 2 == 0
    row = lambda w: pl.BlockSpec((TOKEN_TILE, w), lambda i: (i, 0))
    lo = pl.BlockSpec((TOKEN_TILE, B_WIDTH), lambda i: (
        (i // tiles_per_seq) * half + jnp.minimum(i % tiles_per_seq, half - 1), 0))
    hi = pl.BlockSpec((TOKEN_TILE, B_WIDTH), lambda i: (
        (i // tiles_per_seq) * half + jnp.maximum(i % tiles_per_seq - half, 0), 0))
    vec = _resident((1, D_MODEL))
    return pl.pallas_call(
        functools.partial(_post_kernel, tiles_per_seq=tiles_per_seq),
        grid=(t // TOKEN_TILE,),
        in_specs=[row(D_MODEL), row(A_WIDTH), lo, hi, row(2 * D_MODEL),
                  _resident(wa.shape), _resident(wb.shape), _resident(wo.shape),
                  vec, _resident(wgu.shape), _resident(wdn.shape), vec],
        out_specs=row(D_MODEL),
        out_shape=jax.ShapeDtypeStruct(x1.shape, F32),
        scratch_shapes=[pltpu.VMEM((TOKEN_TILE, D_MODEL), F32)],
        compiler_params=pltpu.CompilerParams(
            dimension_semantics=("arbitrary",), vmem_limit_bytes=VMEM_LIMIT),
        name="post",
    )(x1, oa, ob_lo, ob_hi, gates, wa, wb, wo, g2, wgu, wdn, gf)


def _rotary_tables(seq):
    half = ROT_DIM // 2
    pos = np.arange(seq, dtype=np.float64)
    inv = ROPE_THETA ** (-np.arange(0, ROT_DIM, 2, dtype=np.float64) / ROT_DIM)
    ang = pos[:, None] * inv[None, :]
    cos, sin = np.cos(ang), np.sin(ang)
    ones = np.ones((seq, HEAD_DIM - ROT_DIM))
    zeros = np.zeros((seq, HEAD_DIM - half))
    cos_h = np.concatenate([cos, cos, ones], axis=-1)
    sin_lo = np.concatenate([-sin, zeros], axis=-1)
    sin_hi = np.concatenate([np.zeros((seq, half)), sin,
                             np.zeros((seq, HEAD_DIM - ROT_DIM))], axis=-1)
    two = lambda a: jnp.asarray(np.concatenate([a, a], axis=-1), dtype=F32)
    return two(cos_h), two(sin_lo), two(sin_hi)


def kernel(x, g_ffn1, w_ffn1_gu, w_ffn1_down, g_mix, w_in, qn_a, kn_a, rel_bias, qn_b, kn_b,
           lambda_q1, lambda_k1, lambda_q2, lambda_k2, g_subln, w_up_a, w_up_b, w_out, g_ffn2,
           w_ffn2_gu, w_ffn2_down, g_final):
    b, s, d = x.shape
    assert d == D_MODEL and s % TOKEN_TILE == 0 and g_ffn1.shape[0] == 1
    l = 0
    t = b * s
    two = lambda g: jnp.concatenate([g, g], axis=-1)[None, :]
    cos, sin_lo, sin_hi = _rotary_tables(s)

    x1, w_in_bf16 = _ffn1(x.reshape(t, d), g_ffn1[l][None], w_ffn1_gu[l].astype(BF16),
                          w_ffn1_down[l].astype(BF16), w_in[l])
    (qa, ka, va, qb, kb, vb, gates), post_w = _in_proj(
        x1, g_mix[l][None], w_in_bf16, two(qn_a[l]), two(kn_a[l]), two(qn_b[l]),
        two(kn_b[l]), cos, sin_lo, sin_hi, s,
        [w_up_a[l], w_up_b[l], w_out[l], w_ffn2_gu[l], w_ffn2_down[l]])
    wa, wb, wo, wgu2, wdn2 = post_w
    seq3 = lambda a: a.reshape(b, s, a.shape[-1])
    qk_bound = lambda gq, gk: (1.02 * math.sqrt(HEAD_DIM) * jnp.max(jnp.abs(gq))
                               * jnp.max(jnp.abs(gk))).reshape(1).astype(F32)
    bound_a = qk_bound(qn_a[l], kn_a[l]) + jnp.max(jnp.abs(rel_bias[l])).astype(F32)
    oa = _attn_a(bound_a, seq3(qa), seq3(ka), seq3(va), _clipped_table(rel_bias[l]))
    lam_vecs = jnp.stack([lambda_q1[l], lambda_k1[l], lambda_q2[l], lambda_k2[l]]).astype(F32)
    ob_lo, ob_hi = _attn_b(qk_bound(qn_b[l], kn_b[l]), seq3(qb), seq3(kb), seq3(vb), lam_vecs,
                           g_subln[l][None])
    out = _post(x1, oa.reshape(t, A_WIDTH), ob_lo.reshape(t // 2, B_WIDTH),
                ob_hi.reshape(t // 2, B_WIDTH), gates, wa, wb, wo,
                g_ffn2[l][None], wgu2, wdn2, g_final[l][None], s)
    return out.reshape(b, s, d)
```

```python
import functools
import math

import jax
import jax.numpy as jnp
import numpy as np
from jax import lax
from jax.experimental import pallas as pl
from jax.experimental.pallas import tpu as pltpu

F32 = jnp.float32
BF16 = jnp.bfloat16

D_MODEL = 1024
D_FF = 2816
CHUNK = 64
LEFT_CHUNKS = 8
HEAD_DIM = 64
A_HEADS = 8
A_WIDTH = A_HEADS * HEAD_DIM
REL_CLIP = 256
B_HEADS = 4
B_WIDTH = B_HEADS * 2 * HEAD_DIM
ROPE_THETA = 500000.0
ROT_DIM = HEAD_DIM // 4
EPS = 1e-6
NEG = -1e30
IN_WIDTH = 3 * A_WIDTH + 3 * B_WIDTH + 2 * D_MODEL
LAMBDA_INIT = 0.8 - 0.6 * math.exp(-0.3 * 0)

LANES = 128
CAST_ROWS = 16
TOKEN_TILE = 512
FF_TILE = 256
A_Q_TILE = 256
A_BAND = A_Q_TILE + LEFT_CHUNKS * CHUNK
A_STEP_TILES = 4
A_STRIP_ROWS = 128
A_BIAS_W = 2 * LEFT_CHUNKS * CHUNK + A_BAND
A_TABLE_W = 2048
assert A_TABLE_W >= A_BIAS_W + A_STRIP_ROWS - 1 and A_Q_TILE % A_STRIP_ROWS == 0
B_Q_TILE = 256
B_K_TILE = B_Q_TILE
B_STEP_PAIRS = 1
SHIFT_SAFE = 40.0
VMEM_LIMIT = 56 * 1024 * 1024


def _resident(shape):
    return pl.BlockSpec(shape, lambda *_: (0,) * len(shape), pipeline_mode=pl.Buffered(1))


def _rms(x, g):
    ms = jnp.mean(x * x, axis=-1, keepdims=True)
    return x * lax.rsqrt(ms + EPS) * g


def _swiglu(xn, wgu_ref, wdn_ref, acc_ref):
    for c in range(D_FF // FF_TILE):
        lo = c * FF_TILE
        w_g = wgu_ref[:, lo:lo + FF_TILE].astype(BF16)
        w_u = wgu_ref[:, D_FF + lo:D_FF + lo + FF_TILE].astype(BF16)
        g = jnp.dot(xn, w_g, preferred_element_type=F32)
        u = jnp.dot(xn, w_u, preferred_element_type=F32)
        a = (g * jax.nn.sigmoid(g) * u).astype(BF16)
        d = jnp.dot(a, wdn_ref[lo:lo + FF_TILE, :].astype(BF16), preferred_element_type=F32)
        if c == 0:
            acc_ref[...] = d
        else:
            acc_ref[...] += d


def _ffn1_kernel(x_ref, g_ref, wgu_ref, wdn_ref, *refs):
    n_next = (len(refs) - 2) // 2
    o_ref, acc_ref = refs[n_next], refs[-1]
    x = x_ref[...]
    _swiglu(_rms(x, g_ref[...]).astype(BF16), wgu_ref, wdn_ref, acc_ref)
    o_ref[...] = x + 0.5 * acc_ref[...]
    for src, dst in zip(refs[:n_next], refs[n_next + 1:-1]):
        dst[...] = src[...].astype(BF16)


def _ffn1(x, g, wgu, wdn, w_next):
    t = x.shape[0]
    steps = t // TOKEN_TILE
    row = pl.BlockSpec((TOKEN_TILE, D_MODEL), lambda i: (i, 0))
    assert all(w.shape[0] % (steps * CAST_ROWS) == 0 for w in w_next)
    cast_specs = [pl.BlockSpec((w.shape[0] // steps, w.shape[1]), lambda i: (i, 0))
                  for w in w_next]
    return pl.pallas_call(
        _ffn1_kernel,
        grid=(steps,),
        in_specs=[row, _resident((1, D_MODEL)), _resident(wgu.shape), _resident(wdn.shape)]
        + cast_specs,
        out_specs=[row] + cast_specs,
        out_shape=[jax.ShapeDtypeStruct(x.shape, F32)]
        + [jax.ShapeDtypeStruct(w.shape, BF16) for w in w_next],
        scratch_shapes=[pltpu.VMEM((TOKEN_TILE, D_MODEL), F32)],
        compiler_params=pltpu.CompilerParams(
            dimension_semantics=("arbitrary",), vmem_limit_bytes=VMEM_LIMIT),
        name="ffn1",
    )(x, g, wgu, wdn, *w_next)


def _head_norm(x, g2):
    lane = lax.broadcasted_iota(jnp.int32, x.shape, 1)
    lo = lane < HEAD_DIM
    sq = x * x
    s_lo = jnp.sum(jnp.where(lo, sq, 0.0), axis=-1, keepdims=True)
    s_hi = jnp.sum(jnp.where(lo, 0.0, sq), axis=-1, keepdims=True)
    ms = jnp.where(lo, s_lo, s_hi) * (1.0 / HEAD_DIM)
    return x * lax.rsqrt(ms + EPS) * g2


def _rotary(x, cos, sin_lo, sin_hi):
    half = ROT_DIM // 2
    up = pltpu.roll(x, LANES - half, 1)
    dn = pltpu.roll(x, half, 1)
    return x * cos + up * sin_lo + dn * sin_hi


def _in_proj_kernel(x_ref, g_ref, w_ref, qna_ref, kna_ref, qnb_ref, knb_ref,
                    cos_ref, sl_ref, sh_ref,
                    qa_ref, ka_ref, va_ref, qb_ref, kb_ref, vb_ref, gate_ref):
    h = _rms(x_ref[...], g_ref[...]).astype(BF16)
    scale = 1.0 / math.sqrt(HEAD_DIM)

    def proj(col, width):
        return jnp.dot(h, w_ref[:, col:col + width].astype(BF16), preferred_element_type=F32)

    def normed(col, gain_ref, out_ref, rotate, mult):
        y = proj(col, A_WIDTH)
        for p in range(A_WIDTH // LANES):
            t = _head_norm(y[:, p * LANES:(p + 1) * LANES], gain_ref[...])
            if rotate:
                t = _rotary(t, cos_ref[...], sl_ref[...], sh_ref[...])
            if mult != 1.0:
                t = t * mult
            out_ref[:, p * LANES:(p + 1) * LANES] = t.astype(BF16)

    normed(0 * A_WIDTH, qna_ref, qa_ref, False, scale)
    normed(1 * A_WIDTH, kna_ref, ka_ref, False, 1.0)
    va_ref[...] = proj(2 * A_WIDTH, A_WIDTH).astype(BF16)
    normed(3 * A_WIDTH, qnb_ref, qb_ref, True, scale)
    normed(4 * A_WIDTH, knb_ref, kb_ref, True, 1.0)
    vb_ref[...] = proj(5 * A_WIDTH, B_WIDTH).astype(BF16)
    for c in range(2 * D_MODEL // A_WIDTH):
        gate_ref[:, c * A_WIDTH:(c + 1) * A_WIDTH] = proj(6 * A_WIDTH + c * A_WIDTH, A_WIDTH)


def _in_proj(x1, g, w_in, qna, kna, qnb, knb, cos, sin_lo, sin_hi, seq):
    t = x1.shape[0]
    tiles_per_seq = seq // TOKEN_TILE
    row = lambda w: pl.BlockSpec((TOKEN_TILE, w), lambda i: (i, 0))
    pos = pl.BlockSpec((TOKEN_TILE, LANES), lambda i: (i % tiles_per_seq, 0))
    gain = _resident((1, LANES))
    half = jax.ShapeDtypeStruct((t, A_WIDTH), BF16)
    return pl.pallas_call(
        _in_proj_kernel,
        grid=(t // TOKEN_TILE,),
        in_specs=[row(D_MODEL), _resident((1, D_MODEL)), _resident(w_in.shape),
                  gain, gain, gain, gain, pos, pos, pos],
        out_specs=[row(A_WIDTH)] * 6 + [row(2 * D_MODEL)],
        out_shape=[half] * 6 + [jax.ShapeDtypeStruct((t, 2 * D_MODEL), F32)],
        compiler_params=pltpu.CompilerParams(
            dimension_semantics=("arbitrary",), vmem_limit_bytes=VMEM_LIMIT),
        name="in_proj",
    )(x1, g, w_in, qna, kna, qnb, knb, cos, sin_lo, sin_hi)


def _fill_band_bias(f_ref, bias_ref, shift):
    pad = LEFT_CHUNKS * CHUNK
    shape = (A_STRIP_ROWS, A_BIAS_W)
    q_chunk = lax.broadcasted_iota(jnp.int32, shape, 0) >> 6
    k_chunk = (lax.broadcasted_iota(jnp.int32, shape, 1) - 2 * pad) >> 6
    valid = jnp.logical_and(k_chunk <= q_chunk, k_chunk >= q_chunk - LEFT_CHUNKS)
    for h in range(A_HEADS):
        row = jnp.broadcast_to(f_ref[h:h + 1, :], (A_STRIP_ROWS, A_TABLE_W))
        skew = pltpu.roll(row, 0, 1, stride=1, stride_axis=0)
        bias_ref[h] = jnp.where(valid, skew[:, :A_BIAS_W] - shift, NEG)


def _attn_a_kernel(bound_ref, q_ref, k_ref, v_ref, f_ref, o_ref, bias_ref):
    i = pl.program_id(1)
    pad = LEFT_CHUNKS * CHUNK
    bound = bound_ref[0]
    shift_by_bound = bound <= SHIFT_SAFE

    @pl.when(jnp.logical_and(pl.program_id(0) == 0, i == 0))
    def _():
        _fill_band_bias(f_ref, bias_ref, jnp.where(shift_by_bound, bound, 0.0))

    lane = lax.broadcasted_iota(jnp.int32, (A_Q_TILE, LANES), 1)
    first = lane < HEAD_DIM

    def tile(shifted, sub):
        tile_start = (i * A_STEP_TILES + sub) * A_Q_TILE
        rows = slice(sub * A_Q_TILE, (sub + 1) * A_Q_TILE)
        start = pl.multiple_of(jnp.maximum(tile_start - pad, 0), A_Q_TILE)
        off = 2 * pad - tile_start + start
        for p in range(A_WIDTH // LANES):
            cols = slice(p * LANES, (p + 1) * LANES)
            q = q_ref[0, rows, cols]
            zero = jnp.zeros_like(q)
            qs = jnp.concatenate([jnp.where(first, q, zero), jnp.where(first, zero, q)], axis=0)
            k = k_ref[0, pl.ds(start, A_BAND), cols]
            v = v_ref[0, pl.ds(start, A_BAND), cols]
            s = lax.dot_general(qs, k, (((1,), (1,)), ((), ())), preferred_element_type=F32)
            bias = [bias_ref[2 * p + e, :,
                             pl.ds(pl.multiple_of(off - t * A_STRIP_ROWS, LANES), A_BAND)]
                    for e in range(2) for t in range(A_Q_TILE // A_STRIP_ROWS)]
            s = s + jnp.concatenate(bias, axis=0)
            if not shifted:
                s = s - jnp.max(s, axis=-1, keepdims=True)
            pr = jnp.exp(s)
            l = jnp.sum(pr, axis=-1, keepdims=True)
            o = jnp.dot(pr.astype(BF16), v, preferred_element_type=F32) / l
            o_ref[0, rows, cols] = jnp.where(first, o[:A_Q_TILE], o[A_Q_TILE:]).astype(BF16)

    def step(shifted):
        for sub in range(A_STEP_TILES):
            tile(shifted, sub)

    pl.when(shift_by_bound)(functools.partial(step, True))
    pl.when(jnp.logical_not(shift_by_bound))(functools.partial(step, False))


def _attn_a(bound, q, k, v, f_table):
    b, s, _ = q.shape
    whole = pl.BlockSpec((1, s, A_WIDTH), lambda bi, i: (bi, 0, 0))
    step_rows = A_STEP_TILES * A_Q_TILE
    tile = pl.BlockSpec((1, step_rows, A_WIDTH), lambda bi, i: (bi, i, 0))
    return pl.pallas_call(
        _attn_a_kernel,
        grid=(b, s // step_rows),
        in_specs=[pl.BlockSpec(memory_space=pltpu.SMEM), tile, whole, whole,
                  _resident(f_table.shape)],
        out_specs=tile,
        out_shape=jax.ShapeDtypeStruct(q.shape, BF16),
        scratch_shapes=[pltpu.VMEM((A_HEADS, A_STRIP_ROWS, A_BIAS_W), F32)],
        compiler_params=pltpu.CompilerParams(
            dimension_semantics=("arbitrary", "arbitrary"), vmem_limit_bytes=VMEM_LIMIT),
        name="attn_a",
    )(bound, q, k, v, f_table)


def _clipped_table(rel_table):
    pad = LEFT_CHUNKS * CHUNK
    first, last = rel_table[:1], rel_table[-1:]
    lead = 2 * pad - REL_CLIP
    tail = A_STRIP_ROWS - 1
    mid = A_TABLE_W - lead - rel_table.shape[0] - tail
    f = jnp.concatenate([jnp.repeat(last, lead, axis=0), rel_table[::-1],
                         jnp.repeat(first, mid, axis=0), jnp.repeat(last, tail, axis=0)], axis=0)
    return f.T.astype(F32)


def _attn_b_kernel(bound_ref, q0_ref, q1_ref, k_ref, v_ref, lam_ref, gs_ref, o0_ref, o1_ref,
                   qs_ref, m_ref, l_ref, acc_ref):
    i = pl.program_id(1)
    n_tiles = k_ref.shape[1] // B_Q_TILE
    n_full = n_tiles - 1
    nn = (((1,), (1,)), ((), ()))
    lane = lax.broadcasted_iota(jnp.int32, (B_Q_TILE, LANES), 1)
    first = lane < HEAD_DIM

    def tile_rows(pair, slot):
        pos = pair if slot == 0 else B_STEP_PAIRS - 1 - pair
        return slice(pos * B_Q_TILE, (pos + 1) * B_Q_TILE)

    def stack_queries(pair):
        for slot, q_ref in enumerate((q0_ref, q1_ref)):
            for h in range(B_HEADS):
                q = q_ref[0, tile_rows(pair, slot), h * LANES:(h + 1) * LANES]
                zero = jnp.zeros_like(q)
                qs_ref[2 * pair + slot, h, :B_Q_TILE] = jnp.where(first, q, zero)
                qs_ref[2 * pair + slot, h, B_Q_TILE:] = jnp.where(first, zero, q)

    def scores(slot, h, lo, width, diag_col):
        k = k_ref[0, pl.ds(lo, width), h * LANES:(h + 1) * LANES]
        s = lax.dot_general(qs_ref[slot, h], k, nn, preferred_element_type=F32)
        if diag_col is not None:
            row = lax.broadcasted_iota(jnp.int32, s.shape, 0) & (B_Q_TILE - 1)
            col = lax.broadcasted_iota(jnp.int32, s.shape, 1) - diag_col
            s = jnp.where((col >> 6) <= (row >> 6), s, NEG)
        return [s[:, c * LANES:(c + 1) * LANES] for c in range(width // LANES)]

    def max_block(slot, lo, width, diag_col, assign):
        for h in range(B_HEADS):
            m = functools.reduce(jnp.maximum, scores(slot, h, lo, width, diag_col))
            m_ref[slot, h] = m if assign else jnp.maximum(m_ref[slot, h], m)

    def sum_block(shift, slot, lo, width, diag_col, assign):
        for h in range(B_HEADS):
            m = m_ref[slot, h] if shift is None else shift
            ps = [jnp.exp(s - m) for s in scores(slot, h, lo, width, diag_col)]
            v = v_ref[0, pl.ds(lo, width), h * LANES:(h + 1) * LANES]
            l = functools.reduce(jnp.add, ps)
            pv = jnp.dot(jnp.concatenate(ps, axis=1).astype(BF16), v, preferred_element_type=F32)
            l_ref[slot, h] = l if assign else l_ref[slot, h] + l
            acc_ref[slot, h] = pv if assign else acc_ref[slot, h] + pv

    def sweep(block, unroll, pair):
        a = i * B_STEP_PAIRS + pair
        for slot, tile in enumerate((a, n_tiles - 1 - a)):
            block(2 * pair + slot, pl.multiple_of(tile * B_Q_TILE, B_Q_TILE), B_Q_TILE, 0, True)

        def full(t):
            in_slot0 = t < a
            lo = jnp.where(in_slot0, t, t - a) * B_K_TILE
            block(2 * pair + jnp.where(in_slot0, 0, 1), pl.multiple_of(lo, B_K_TILE), B_K_TILE,
                  None, False)

        if unroll:
            for t in range(n_full):
                full(t)
        else:
            def body(t, carry):
                full(t)
                return carry
            lax.fori_loop(0, n_full, body, 0)

    bound = bound_ref[0]
    shift_by_bound = bound <= SHIFT_SAFE

    def finish(pair):
        lq1, lk1, lq2, lk2 = (lam_ref[n:n + 1, :] for n in range(4))
        lam = (jnp.exp(jnp.sum(lq1 * lk1, axis=-1, keepdims=True))
               - jnp.exp(jnp.sum(lq2 * lk2, axis=-1, keepdims=True)) + LAMBDA_INIT)
        for slot, o_ref in enumerate((o0_ref, o1_ref)):
            idx = 2 * pair + slot
            for h in range(B_HEADS):
                o = acc_ref[idx, h] / jnp.sum(l_ref[idx, h], axis=-1, keepdims=True)
                o = o[:B_Q_TILE] - lam * o[B_Q_TILE:]
                o_ref[0, tile_rows(pair, slot), h * LANES:(h + 1) * LANES] = (
                    _rms(o, gs_ref[...]) * (1.0 - LAMBDA_INIT)).astype(BF16)

    @pl.when(shift_by_bound)
    def _():
        for pair in range(B_STEP_PAIRS):
            stack_queries(pair)
            sweep(functools.partial(sum_block, bound), True, pair)
            finish(pair)

    @pl.when(jnp.logical_not(shift_by_bound))
    def _():
        for pair in range(B_STEP_PAIRS):
            stack_queries(pair)
            sweep(max_block, False, pair)
            for idx in (2 * pair, 2 * pair + 1):
                for h in range(B_HEADS):
                    m_ref[idx, h] = jnp.broadcast_to(
                        jnp.max(m_ref[idx, h], axis=-1, keepdims=True), m_ref.shape[2:])
            sweep(functools.partial(sum_block, None), False, pair)
            finish(pair)


def _attn_b(bound, q, k, v, lam_vecs, g_sub):
    b, s, _ = q.shape
    n_tiles = s // B_Q_TILE
    assert n_tiles % (2 * B_STEP_PAIRS) == 0 and B_K_TILE == B_Q_TILE
    steps = n_tiles // (2 * B_STEP_PAIRS)
    rows = B_STEP_PAIRS * B_Q_TILE
    whole = pl.BlockSpec((1, s, B_WIDTH), lambda bi, i: (bi, 0, 0))
    q_lo = pl.BlockSpec((1, rows, B_WIDTH), lambda bi, i: (bi, i, 0))
    q_hi = pl.BlockSpec((1, rows, B_WIDTH), lambda bi, i: (bi, 2 * steps - 1 - i, 0))
    o_hi = pl.BlockSpec((1, rows, B_WIDTH), lambda bi, i: (bi, steps - 1 - i, 0))
    stat = pltpu.VMEM((2 * B_STEP_PAIRS, B_HEADS, 2 * B_Q_TILE, LANES), F32)
    out = jax.ShapeDtypeStruct((b, s // 2, B_WIDTH), BF16)
    return pl.pallas_call(
        _attn_b_kernel,
        grid=(b, steps),
        in_specs=[pl.BlockSpec(memory_space=pltpu.SMEM), q_lo, q_hi, whole, whole,
                  _resident(lam_vecs.shape), _resident(g_sub.shape)],
        out_specs=[q_lo, o_hi],
        out_shape=[out, out],
        scratch_shapes=[pltpu.VMEM((2 * B_STEP_PAIRS, B_HEADS, 2 * B_Q_TILE, LANES), BF16),
                        stat, stat, stat],
        compiler_params=pltpu.CompilerParams(
            dimension_semantics=("arbitrary", "arbitrary"), vmem_limit_bytes=VMEM_LIMIT),
        name="attn_b",
    )(bound, q, q, k, v, lam_vecs, g_sub)


def _post_kernel(x_ref, oa_ref, ob_lo_ref, ob_hi_ref, gate_ref, wa_ref, wb_ref, wo_ref,
                 g2_ref, wgu_ref, wdn_ref, gf_ref, o_ref, acc_ref, *, tiles_per_seq):
    ya = jnp.dot(oa_ref[...], wa_ref[...].astype(BF16), preferred_element_type=F32)
    in_lo = (pl.program_id(0) % tiles_per_seq) < tiles_per_seq // 2
    ob = jnp.where(in_lo, ob_lo_ref[...], ob_hi_ref[...])
    yb = jnp.dot(ob, wb_ref[...].astype(BF16), preferred_element_type=F32)
    y = (jax.nn.sigmoid(gate_ref[:, :D_MODEL]) * ya
         + jax.nn.sigmoid(gate_ref[:, D_MODEL:]) * yb)
    x2 = x_ref[...] + jnp.dot(y.astype(BF16), wo_ref[...], preferred_element_type=F32)
    _swiglu(_rms(x2, g2_ref[...]).astype(BF16), wgu_ref, wdn_ref, acc_ref)
    o_ref[...] = _rms(x2 + 0.5 * acc_ref[...], gf_ref[...])


def _post(x1, oa, ob_lo, ob_hi, gates, wa, wb, wo, g2, wgu, wdn, gf, seq):
    t = x1.shape[0]
    tiles_per_seq = seq // TOKEN_TILE
    half = tiles_per_seq // 2
    assert tiles_per_seq % 2 == 0
    row = lambda w: pl.BlockSpec((TOKEN_TILE, w), lambda i: (i, 0))
    lo = pl.BlockSpec((TOKEN_TILE, B_WIDTH), lambda i: (
        (i // tiles_per_seq) * half + jnp.minimum(i % tiles_per_seq, half - 1), 0))
    hi = pl.BlockSpec((TOKEN_TILE, B_WIDTH), lambda i: (
        (i // tiles_per_seq) * half + jnp.maximum(i % tiles_per_seq - half, 0), 0))
    vec = _resident((1, D_MODEL))
    return pl.pallas_call(
        functools.partial(_post_kernel, tiles_per_seq=tiles_per_seq),
        grid=(t // TOKEN_TILE,),
        in_specs=[row(D_MODEL), row(A_WIDTH), lo, hi, row(2 * D_MODEL),
                  _resident(wa.shape), _resident(wb.shape), _resident(wo.shape),
                  vec, _resident(wgu.shape), _resident(wdn.shape), vec],
        out_specs=row(D_MODEL),
        out_shape=jax.ShapeDtypeStruct(x1.shape, F32),
        scratch_shapes=[pltpu.VMEM((TOKEN_TILE, D_MODEL), F32)],
        compiler_params=pltpu.CompilerParams(
            dimension_semantics=("arbitrary",), vmem_limit_bytes=VMEM_LIMIT),
        name="post",
    )(x1, oa, ob_lo, ob_hi, gates, wa, wb, wo, g2, wgu, wdn, gf)


def _rotary_tables(seq):
    half = ROT_DIM // 2
    pos = np.arange(seq, dtype=np.float64)
    inv = ROPE_THETA ** (-np.arange(0, ROT_DIM, 2, dtype=np.float64) / ROT_DIM)
    ang = pos[:, None] * inv[None, :]
    cos, sin = np.cos(ang), np.sin(ang)
    ones = np.ones((seq, HEAD_DIM - ROT_DIM))
    zeros = np.zeros((seq, HEAD_DIM - half))
    cos_h = np.concatenate([cos, cos, ones], axis=-1)
    sin_lo = np.concatenate([-sin, zeros], axis=-1)
    sin_hi = np.concatenate([np.zeros((seq, half)), sin,
                             np.zeros((seq, HEAD_DIM - ROT_DIM))], axis=-1)
    two = lambda a: jnp.asarray(np.concatenate([a, a], axis=-1), dtype=F32)
    return two(cos_h), two(sin_lo), two(sin_hi)


def kernel(x, g_ffn1, w_ffn1_gu, w_ffn1_down, g_mix, w_in, qn_a, kn_a, rel_bias, qn_b, kn_b,
           lambda_q1, lambda_k1, lambda_q2, lambda_k2, g_subln, w_up_a, w_up_b, w_out, g_ffn2,
           w_ffn2_gu, w_ffn2_down, g_final):
    b, s, d = x.shape
    assert d == D_MODEL and s % TOKEN_TILE == 0 and g_ffn1.shape[0] == 1
    l = 0
    t = b * s
    two = lambda g: jnp.concatenate([g, g], axis=-1)[None, :]
    cos, sin_lo, sin_hi = _rotary_tables(s)

    x1, wo, wgu2 = _ffn1(x.reshape(t, d), g_ffn1[l][None], w_ffn1_gu[l], w_ffn1_down[l],
                         [w_out[l], w_ffn2_gu[l]])
    qa, ka, va, qb, kb, vb, gates = _in_proj(
        x1, g_mix[l][None], w_in[l], two(qn_a[l]), two(kn_a[l]), two(qn_b[l]),
        two(kn_b[l]), cos, sin_lo, sin_hi, s)
    seq3 = lambda a: a.reshape(b, s, a.shape[-1])
    qk_bound = lambda gq, gk: (1.02 * math.sqrt(HEAD_DIM) * jnp.max(jnp.abs(gq))
                               * jnp.max(jnp.abs(gk))).reshape(1).astype(F32)
    bound_a = qk_bound(qn_a[l], kn_a[l]) + jnp.max(jnp.abs(rel_bias[l])).astype(F32)
    oa = _attn_a(bound_a, seq3(qa), seq3(ka), seq3(va), _clipped_table(rel_bias[l]))
    lam_vecs = jnp.stack([lambda_q1[l], lambda_k1[l], lambda_q2[l], lambda_k2[l]]).astype(F32)
    ob_lo, ob_hi = _attn_b(qk_bound(qn_b[l], kn_b[l]), seq3(qb), seq3(kb), seq3(vb), lam_vecs,
                           g_subln[l][None])
    out = _post(x1, oa.reshape(t, A_WIDTH), ob_lo.reshape(t // 2, B_WIDTH),
                ob_hi.reshape(t // 2, B_WIDTH), gates, w_up_a[l], w_up_b[l], wo,
                g_ffn2[l][None], wgu2, w_ffn2_down[l], g_final[l][None], s)
    return out.reshape(b, s, d)
```

```python
import functools
import math

import jax
import jax.numpy as jnp
import numpy as np
from jax import lax
from jax.experimental import pallas as pl
from jax.experimental.pallas import tpu as pltpu

F32 = jnp.float32
BF16 = jnp.bfloat16

D_MODEL = 1024
D_FF = 2816
CHUNK = 64
LEFT_CHUNKS = 8
HEAD_DIM = 64
A_HEADS = 8
A_WIDTH = A_HEADS * HEAD_DIM
REL_CLIP = 256
B_HEADS = 4
B_WIDTH = B_HEADS * 2 * HEAD_DIM
ROPE_THETA = 500000.0
ROT_DIM = HEAD_DIM // 4
EPS = 1e-6
NEG = -1e30
IN_WIDTH = 3 * A_WIDTH + 3 * B_WIDTH + 2 * D_MODEL
LAMBDA_INIT = 0.8 - 0.6 * math.exp(-0.3 * 0)

LANES = 128
CAST_ROWS = 16
TOKEN_TILE = 512
FF_TILE = 256
A_Q_TILE = 256
A_BAND = A_Q_TILE + LEFT_CHUNKS * CHUNK
A_STEP_TILES = 4
A_STRIP_ROWS = 128
A_BIAS_W = 2 * LEFT_CHUNKS * CHUNK + A_BAND
A_TABLE_W = 2048
assert A_TABLE_W >= A_BIAS_W + A_STRIP_ROWS - 1 and A_Q_TILE % A_STRIP_ROWS == 0
B_Q_TILE = 256
B_K_TILE = B_Q_TILE
B_STEP_PAIRS = 1
SHIFT_SAFE = 40.0
VMEM_LIMIT = 56 * 1024 * 1024


def _resident(shape):
    return pl.BlockSpec(shape, lambda *_: (0,) * len(shape), pipeline_mode=pl.Buffered(1))


def _rms(x, g):
    ms = jnp.mean(x * x, axis=-1, keepdims=True)
    return x * lax.rsqrt(ms + EPS) * g


def _swiglu(xn, wgu_ref, wdn_ref, acc_ref):
    for c in range(D_FF // FF_TILE):
        lo = c * FF_TILE
        w_g = wgu_ref[:, lo:lo + FF_TILE].astype(BF16)
        w_u = wgu_ref[:, D_FF + lo:D_FF + lo + FF_TILE].astype(BF16)
        g = jnp.dot(xn, w_g, preferred_element_type=F32)
        u = jnp.dot(xn, w_u, preferred_element_type=F32)
        a = (g * jax.nn.sigmoid(g) * u).astype(BF16)
        d = jnp.dot(a, wdn_ref[lo:lo + FF_TILE, :].astype(BF16), preferred_element_type=F32)
        if c == 0:
            acc_ref[...] = d
        else:
            acc_ref[...] += d


def _ffn1_kernel(x_ref, g_ref, wgu_ref, wdn_ref, *refs):
    n_next = (len(refs) - 2) // 2
    o_ref, acc_ref = refs[n_next], refs[-1]
    x = x_ref[...]
    _swiglu(_rms(x, g_ref[...]).astype(BF16), wgu_ref, wdn_ref, acc_ref)
    o_ref[...] = x + 0.5 * acc_ref[...]
    for src, dst in zip(refs[:n_next], refs[n_next + 1:-1]):
        dst[...] = src[...].astype(BF16)


def _ffn1(x, g, wgu, wdn, w_next):
    t = x.shape[0]
    steps = t // TOKEN_TILE
    row = pl.BlockSpec((TOKEN_TILE, D_MODEL), lambda i: (i, 0))
    assert all(w.shape[0] % (steps * CAST_ROWS) == 0 for w in w_next)
    cast_specs = [pl.BlockSpec((w.shape[0] // steps, w.shape[1]), lambda i: (i, 0))
                  for w in w_next]
    return pl.pallas_call(
        _ffn1_kernel,
        grid=(steps,),
        in_specs=[row, _resident((1, D_MODEL)), _resident(wgu.shape), _resident(wdn.shape)]
        + cast_specs,
        out_specs=[row] + cast_specs,
        out_shape=[jax.ShapeDtypeStruct(x.shape, F32)]
        + [jax.ShapeDtypeStruct(w.shape, BF16) for w in w_next],
        scratch_shapes=[pltpu.VMEM((TOKEN_TILE, D_MODEL), F32)],
        compiler_params=pltpu.CompilerParams(
            dimension_semantics=("arbitrary",), vmem_limit_bytes=VMEM_LIMIT),
        name="ffn1",
    )(x, g, wgu, wdn, *w_next)


def _head_norm(x, g2):
    lane = lax.broadcasted_iota(jnp.int32, x.shape, 1)
    lo = lane < HEAD_DIM
    sq = x * x
    s_lo = jnp.sum(jnp.where(lo, sq, 0.0), axis=-1, keepdims=True)
    s_hi = jnp.sum(jnp.where(lo, 0.0, sq), axis=-1, keepdims=True)
    ms = jnp.where(lo, s_lo, s_hi) * (1.0 / HEAD_DIM)
    return x * lax.rsqrt(ms + EPS) * g2


def _rotary(x, cos, sin_lo, sin_hi):
    half = ROT_DIM // 2
    up = pltpu.roll(x, LANES - half, 1)
    dn = pltpu.roll(x, half, 1)
    return x * cos + up * sin_lo + dn * sin_hi


def _in_proj_kernel(x_ref, g_ref, w_ref, qna_ref, kna_ref, qnb_ref, knb_ref,
                    cos_ref, sl_ref, sh_ref,
                    qa_ref, ka_ref, va_ref, qb_ref, kb_ref, vb_ref, gate_ref):
    h = _rms(x_ref[...], g_ref[...]).astype(BF16)
    scale = 1.0 / math.sqrt(HEAD_DIM)

    def proj(col, width):
        return jnp.dot(h, w_ref[:, col:col + width].astype(BF16), preferred_element_type=F32)

    def normed(col, gain_ref, out_ref, rotate, mult):
        y = proj(col, A_WIDTH)
        for p in range(A_WIDTH // LANES):
            t = _head_norm(y[:, p * LANES:(p + 1) * LANES], gain_ref[...])
            if rotate:
                t = _rotary(t, cos_ref[...], sl_ref[...], sh_ref[...])
            if mult != 1.0:
                t = t * mult
            out_ref[:, p * LANES:(p + 1) * LANES] = t.astype(BF16)

    normed(0 * A_WIDTH, qna_ref, qa_ref, False, scale)
    normed(1 * A_WIDTH, kna_ref, ka_ref, False, 1.0)
    va_ref[...] = proj(2 * A_WIDTH, A_WIDTH).astype(BF16)
    normed(3 * A_WIDTH, qnb_ref, qb_ref, True, scale)
    normed(4 * A_WIDTH, knb_ref, kb_ref, True, 1.0)
    vb_ref[...] = proj(5 * A_WIDTH, B_WIDTH).astype(BF16)
    for c in range(2 * D_MODEL // A_WIDTH):
        gate_ref[:, c * A_WIDTH:(c + 1) * A_WIDTH] = proj(6 * A_WIDTH + c * A_WIDTH, A_WIDTH)


def _in_proj(x1, g, w_in, qna, kna, qnb, knb, cos, sin_lo, sin_hi, seq):
    t = x1.shape[0]
    tiles_per_seq = seq // TOKEN_TILE
    row = lambda w: pl.BlockSpec((TOKEN_TILE, w), lambda i: (i, 0))
    pos = pl.BlockSpec((TOKEN_TILE, LANES), lambda i: (i % tiles_per_seq, 0))
    gain = _resident((1, LANES))
    half = jax.ShapeDtypeStruct((t, A_WIDTH), BF16)
    return pl.pallas_call(
        _in_proj_kernel,
        grid=(t // TOKEN_TILE,),
        in_specs=[row(D_MODEL), _resident((1, D_MODEL)), _resident(w_in.shape),
                  gain, gain, gain, gain, pos, pos, pos],
        out_specs=[row(A_WIDTH)] * 6 + [row(2 * D_MODEL)],
        out_shape=[half] * 6 + [jax.ShapeDtypeStruct((t, 2 * D_MODEL), F32)],
        compiler_params=pltpu.CompilerParams(
            dimension_semantics=("arbitrary",), vmem_limit_bytes=VMEM_LIMIT),
        name="in_proj",
    )(x1, g, w_in, qna, kna, qnb, knb, cos, sin_lo, sin_hi)


def _fill_band_bias(f_ref, bias_ref, shift):
    pad = LEFT_CHUNKS * CHUNK
    shape = (A_STRIP_ROWS, A_BIAS_W)
    q_chunk = lax.broadcasted_iota(jnp.int32, shape, 0) >> 6
    k_chunk = (lax.broadcasted_iota(jnp.int32, shape, 1) - 2 * pad) >> 6
    valid = jnp.logical_and(k_chunk <= q_chunk, k_chunk >= q_chunk - LEFT_CHUNKS)
    for h in range(A_HEADS):
        row = jnp.broadcast_to(f_ref[h:h + 1, :], (A_STRIP_ROWS, A_TABLE_W))
        skew = pltpu.roll(row, 0, 1, stride=1, stride_axis=0)
        bias_ref[h] = jnp.where(valid, skew[:, :A_BIAS_W] - shift, NEG)


def _attn_a_kernel(bound_ref, q_ref, k_ref, v_ref, f_ref, o_ref, bias_ref):
    i = pl.program_id(1)
    pad = LEFT_CHUNKS * CHUNK
    bound = bound_ref[0]
    shift_by_bound = bound <= SHIFT_SAFE

    @pl.when(jnp.logical_and(pl.program_id(0) == 0, i == 0))
    def _():
        _fill_band_bias(f_ref, bias_ref, jnp.where(shift_by_bound, bound, 0.0))

    lane = lax.broadcasted_iota(jnp.int32, (A_Q_TILE, LANES), 1)
    first = lane < HEAD_DIM

    def tile(shifted, sub):
        tile_start = (i * A_STEP_TILES + sub) * A_Q_TILE
        rows = slice(sub * A_Q_TILE, (sub + 1) * A_Q_TILE)
        start = pl.multiple_of(jnp.maximum(tile_start - pad, 0), A_Q_TILE)
        off = 2 * pad - tile_start + start
        for p in range(A_WIDTH // LANES):
            cols = slice(p * LANES, (p + 1) * LANES)
            q = q_ref[0, rows, cols]
            zero = jnp.zeros_like(q)
            qs = jnp.concatenate([jnp.where(first, q, zero), jnp.where(first, zero, q)], axis=0)
            k = k_ref[0, pl.ds(start, A_BAND), cols]
            v = v_ref[0, pl.ds(start, A_BAND), cols]
            s = lax.dot_general(qs, k, (((1,), (1,)), ((), ())), preferred_element_type=F32)
            bias = [bias_ref[2 * p + e, :,
                             pl.ds(pl.multiple_of(off - t * A_STRIP_ROWS, LANES), A_BAND)]
                    for e in range(2) for t in range(A_Q_TILE // A_STRIP_ROWS)]
            s = s + jnp.concatenate(bias, axis=0)
            if not shifted:
                s = s - jnp.max(s, axis=-1, keepdims=True)
            pr = jnp.exp(s)
            l = jnp.sum(pr, axis=-1, keepdims=True)
            o = jnp.dot(pr.astype(BF16), v, preferred_element_type=F32) / l
            o_ref[0, rows, cols] = jnp.where(first, o[:A_Q_TILE], o[A_Q_TILE:]).astype(BF16)

    def step(shifted):
        for sub in range(A_STEP_TILES):
            tile(shifted, sub)

    pl.when(shift_by_bound)(functools.partial(step, True))
    pl.when(jnp.logical_not(shift_by_bound))(functools.partial(step, False))


def _attn_a(bound, q, k, v, f_table):
    b, s, _ = q.shape
    whole = pl.BlockSpec((1, s, A_WIDTH), lambda bi, i: (bi, 0, 0))
    step_rows = A_STEP_TILES * A_Q_TILE
    tile = pl.BlockSpec((1, step_rows, A_WIDTH), lambda bi, i: (bi, i, 0))
    return pl.pallas_call(
        _attn_a_kernel,
        grid=(b, s // step_rows),
        in_specs=[pl.BlockSpec(memory_space=pltpu.SMEM), tile, whole, whole,
                  _resident(f_table.shape)],
        out_specs=tile,
        out_shape=jax.ShapeDtypeStruct(q.shape, BF16),
        scratch_shapes=[pltpu.VMEM((A_HEADS, A_STRIP_ROWS, A_BIAS_W), F32)],
        compiler_params=pltpu.CompilerParams(
            dimension_semantics=("arbitrary", "arbitrary"), vmem_limit_bytes=VMEM_LIMIT),
        name="attn_a",
    )(bound, q, k, v, f_table)


def _clipped_table(rel_table):
    pad = LEFT_CHUNKS * CHUNK
    first, last = rel_table[:1], rel_table[-1:]
    lead = 2 * pad - REL_CLIP
    tail = A_STRIP_ROWS - 1
    mid = A_TABLE_W - lead - rel_table.shape[0] - tail
    f = jnp.concatenate([jnp.repeat(last, lead, axis=0), rel_table[::-1],
                         jnp.repeat(first, mid, axis=0), jnp.repeat(last, tail, axis=0)], axis=0)
    return f.T.astype(F32)


def _attn_b_kernel(bound_ref, q0_ref, q1_ref, k_ref, v_ref, lam_ref, gs_ref, o0_ref, o1_ref,
                   qs_ref, m_ref, l_ref, acc_ref):
    i = pl.program_id(1)
    n_tiles = k_ref.shape[1] // B_Q_TILE
    n_full = n_tiles - 1
    nn = (((1,), (1,)), ((), ()))
    lane = lax.broadcasted_iota(jnp.int32, (B_Q_TILE, LANES), 1)
    first = lane < HEAD_DIM

    def tile_rows(pair, slot):
        pos = pair if slot == 0 else B_STEP_PAIRS - 1 - pair
        return slice(pos * B_Q_TILE, (pos + 1) * B_Q_TILE)

    def stack_queries(pair):
        for slot, q_ref in enumerate((q0_ref, q1_ref)):
            for h in range(B_HEADS):
                q = q_ref[0, tile_rows(pair, slot), h * LANES:(h + 1) * LANES]
                zero = jnp.zeros_like(q)
                qs_ref[2 * pair + slot, h, :B_Q_TILE] = jnp.where(first, q, zero)
                qs_ref[2 * pair + slot, h, B_Q_TILE:] = jnp.where(first, zero, q)

    def scores(slot, h, lo, width, diag_col):
        k = k_ref[0, pl.ds(lo, width), h * LANES:(h + 1) * LANES]
        s = lax.dot_general(qs_ref[slot, h], k, nn, preferred_element_type=F32)
        if diag_col is not None:
            row = lax.broadcasted_iota(jnp.int32, s.shape, 0) & (B_Q_TILE - 1)
            col = lax.broadcasted_iota(jnp.int32, s.shape, 1) - diag_col
            s = jnp.where((col >> 6) <= (row >> 6), s, NEG)
        return [s[:, c * LANES:(c + 1) * LANES] for c in range(width // LANES)]

    all_heads = tuple(range(B_HEADS))

    def max_block(slot, lo, width, diag_col, assign, heads=all_heads):
        for h in heads:
            m = functools.reduce(jnp.maximum, scores(slot, h, lo, width, diag_col))
            m_ref[slot, h] = m if assign else jnp.maximum(m_ref[slot, h], m)

    def sum_block(shift, slot, lo, width, diag_col, assign, heads=all_heads):
        for h in heads:
            m = m_ref[slot, h] if shift is None else shift
            ps = [jnp.exp(s - m) for s in scores(slot, h, lo, width, diag_col)]
            v = v_ref[0, pl.ds(lo, width), h * LANES:(h + 1) * LANES]
            l = functools.reduce(jnp.add, ps)
            pv = jnp.dot(jnp.concatenate(ps, axis=1).astype(BF16), v, preferred_element_type=F32)
            l_ref[slot, h] = l if assign else l_ref[slot, h] + l
            acc_ref[slot, h] = pv if assign else acc_ref[slot, h] + pv

    def sweep(block, unroll, pair, slot_done=None):
        a = i * B_STEP_PAIRS + pair
        for slot, tile in enumerate((a, n_tiles - 1 - a)):
            block(2 * pair + slot, pl.multiple_of(tile * B_Q_TILE, B_Q_TILE), B_Q_TILE, 0, True)

        def full(t):
            in_slot0 = t < a
            lo = jnp.where(in_slot0, t, t - a) * B_K_TILE
            block(2 * pair + jnp.where(in_slot0, 0, 1), pl.multiple_of(lo, B_K_TILE), B_K_TILE,
                  None, False)

        if unroll:
            shared = n_tiles // 2 - 1
            for t in range(shared):
                full(t)
            slot_done(0, all_heads)
            for h in all_heads:
                for t in range(shared, n_full):
                    block(2 * pair + 1, pl.multiple_of((t - a) * B_K_TILE, B_K_TILE), B_K_TILE,
                          None, False, heads=(h,))
                slot_done(1, (h,))
        else:
            def body(t, carry):
                full(t)
                return carry
            lax.fori_loop(0, n_full, body, 0)

    bound = bound_ref[0]
    shift_by_bound = bound <= SHIFT_SAFE

    def finish(pair, slot, heads=all_heads):
        lq1, lk1, lq2, lk2 = (lam_ref[n:n + 1, :] for n in range(4))
        lam = (jnp.exp(jnp.sum(lq1 * lk1, axis=-1, keepdims=True))
               - jnp.exp(jnp.sum(lq2 * lk2, axis=-1, keepdims=True)) + LAMBDA_INIT)
        o_ref = (o0_ref, o1_ref)[slot]
        idx = 2 * pair + slot
        for h in heads:
            o = acc_ref[idx, h] / jnp.sum(l_ref[idx, h], axis=-1, keepdims=True)
            o = o[:B_Q_TILE] - lam * o[B_Q_TILE:]
            o_ref[0, tile_rows(pair, slot), h * LANES:(h + 1) * LANES] = (
                _rms(o, gs_ref[...]) * (1.0 - LAMBDA_INIT)).astype(BF16)

    @pl.when(shift_by_bound)
    def _():
        for pair in range(B_STEP_PAIRS):
            stack_queries(pair)
            sweep(functools.partial(sum_block, bound), True, pair,
                  slot_done=functools.partial(finish, pair))

    @pl.when(jnp.logical_not(shift_by_bound))
    def _():
        for pair in range(B_STEP_PAIRS):
            stack_queries(pair)
            sweep(max_block, False, pair)
            for idx in (2 * pair, 2 * pair + 1):
                for h in range(B_HEADS):
                    m_ref[idx, h] = jnp.broadcast_to(
                        jnp.max(m_ref[idx, h], axis=-1, keepdims=True), m_ref.shape[2:])
            sweep(functools.partial(sum_block, None), False, pair)
            finish(pair, 0)
            finish(pair, 1)


def _attn_b(bound, q, k, v, lam_vecs, g_sub):
    b, s, _ = q.shape
    n_tiles = s // B_Q_TILE
    assert n_tiles % (2 * B_STEP_PAIRS) == 0 and B_K_TILE == B_Q_TILE
    steps = n_tiles // (2 * B_STEP_PAIRS)
    rows = B_STEP_PAIRS * B_Q_TILE
    whole = pl.BlockSpec((1, s, B_WIDTH), lambda bi, i: (bi, 0, 0))
    q_lo = pl.BlockSpec((1, rows, B_WIDTH), lambda bi, i: (bi, i, 0))
    q_hi = pl.BlockSpec((1, rows, B_WIDTH), lambda bi, i: (bi, 2 * steps - 1 - i, 0))
    o_hi = pl.BlockSpec((1, rows, B_WIDTH), lambda bi, i: (bi, steps - 1 - i, 0))
    stat = pltpu.VMEM((2 * B_STEP_PAIRS, B_HEADS, 2 * B_Q_TILE, LANES), F32)
    out = jax.ShapeDtypeStruct((b, s // 2, B_WIDTH), BF16)
    return pl.pallas_call(
        _attn_b_kernel,
        grid=(b, steps),
        in_specs=[pl.BlockSpec(memory_space=pltpu.SMEM), q_lo, q_hi, whole, whole,
                  _resident(lam_vecs.shape), _resident(g_sub.shape)],
        out_specs=[q_lo, o_hi],
        out_shape=[out, out],
        scratch_shapes=[pltpu.VMEM((2 * B_STEP_PAIRS, B_HEADS, 2 * B_Q_TILE, LANES), BF16),
                        stat, stat, stat],
        compiler_params=pltpu.CompilerParams(
            dimension_semantics=("arbitrary", "arbitrary"), vmem_limit_bytes=VMEM_LIMIT),
        name="attn_b",
    )(bound, q, q, k, v, lam_vecs, g_sub)


def _post_kernel(x_ref, oa_ref, ob_lo_ref, ob_hi_ref, gate_ref, wa_ref, wb_ref, wo_ref,
                 g2_ref, wgu_ref, wdn_ref, gf_ref, o_ref, acc_ref, *, tiles_per_seq):
    ya = jnp.dot(oa_ref[...], wa_ref[...].astype(BF16), preferred_element_type=F32)
    in_lo = (pl.program_id(0) % tiles_per_seq) < tiles_per_seq // 2
    ob = jnp.where(in_lo, ob_lo_ref[...], ob_hi_ref[...])
    yb = jnp.dot(ob, wb_ref[...].astype(BF16), preferred_element_type=F32)
    y = (jax.nn.sigmoid(gate_ref[:, :D_MODEL]) * ya
         + jax.nn.sigmoid(gate_ref[:, D_MODEL:]) * yb)
    x2 = x_ref[...] + jnp.dot(y.astype(BF16), wo_ref[...], preferred_element_type=F32)
    _swiglu(_rms(x2, g2_ref[...]).astype(BF16), wgu_ref, wdn_ref, acc_ref)
    o_ref[...] = _rms(x2 + 0.5 * acc_ref[...], gf_ref[...])


def _post(x1, oa, ob_lo, ob_hi, gates, wa, wb, wo, g2, wgu, wdn, gf, seq):
    t = x1.shape[0]
    tiles_per_seq = seq // TOKEN_TILE
    half = tiles_per_seq // 2
    assert tiles_per_seq % 2 == 0
    row = lambda w: pl.BlockSpec((TOKEN_TILE, w), lambda i: (i, 0))
    lo = pl.BlockSpec((TOKEN_TILE, B_WIDTH), lambda i: (
        (i // tiles_per_seq) * half + jnp.minimum(i % tiles_per_seq, half - 1), 0))
    hi = pl.BlockSpec((TOKEN_TILE, B_WIDTH), lambda i: (
        (i // tiles_per_seq) * half + jnp.maximum(i % tiles_per_seq - half, 0), 0))
    vec = _resident((1, D_MODEL))
    return pl.pallas_call(
        functools.partial(_post_kernel, tiles_per_seq=tiles_per_seq),
        grid=(t // TOKEN_TILE,),
        in_specs=[row(D_MODEL), row(A_WIDTH), lo, hi, row(2 * D_MODEL),
                  _resident(wa.shape), _resident(wb.shape), _resident(wo.shape),
                  vec, _resident(wgu.shape), _resident(wdn.shape), vec],
        out_specs=row(D_MODEL),
        out_shape=jax.ShapeDtypeStruct(x1.shape, F32),
        scratch_shapes=[pltpu.VMEM((TOKEN_TILE, D_MODEL), F32)],
        compiler_params=pltpu.CompilerParams(
            dimension_semantics=("arbitrary",), vmem_limit_bytes=VMEM_LIMIT),
        name="post",
    )(x1, oa, ob_lo, ob_hi, gates, wa, wb, wo, g2, wgu, wdn, gf)


def _rotary_tables(seq):
    half = ROT_DIM // 2
    pos = np.arange(seq, dtype=np.float64)
    inv = ROPE_THETA ** (-np.arange(0, ROT_DIM, 2, dtype=np.float64) / ROT_DIM)
    ang = pos[:, None] * inv[None, :]
    cos, sin = np.cos(ang), np.sin(ang)
    ones = np.ones((seq, HEAD_DIM - ROT_DIM))
    zeros = np.zeros((seq, HEAD_DIM - half))
    cos_h = np.concatenate([cos, cos, ones], axis=-1)
    sin_lo = np.concatenate([-sin, zeros], axis=-1)
    sin_hi = np.concatenate([np.zeros((seq, half)), sin,
                             np.zeros((seq, HEAD_DIM - ROT_DIM))], axis=-1)
    two = lambda a: jnp.asarray(np.concatenate([a, a], axis=-1), dtype=F32)
    return two(cos_h), two(sin_lo), two(sin_hi)


def kernel(x, g_ffn1, w_ffn1_gu, w_ffn1_down, g_mix, w_in, qn_a, kn_a, rel_bias, qn_b, kn_b,
           lambda_q1, lambda_k1, lambda_q2, lambda_k2, g_subln, w_up_a, w_up_b, w_out, g_ffn2,
           w_ffn2_gu, w_ffn2_down, g_final):
    b, s, d = x.shape
    assert d == D_MODEL and s % TOKEN_TILE == 0 and g_ffn1.shape[0] == 1
    l = 0
    t = b * s
    two = lambda g: jnp.concatenate([g, g], axis=-1)[None, :]
    cos, sin_lo, sin_hi = _rotary_tables(s)

    x1, wo, wgu2 = _ffn1(x.reshape(t, d), g_ffn1[l][None], w_ffn1_gu[l], w_ffn1_down[l],
                         [w_out[l], w_ffn2_gu[l]])
    qa, ka, va, qb, kb, vb, gates = _in_proj(
        x1, g_mix[l][None], w_in[l], two(qn_a[l]), two(kn_a[l]), two(qn_b[l]),
        two(kn_b[l]), cos, sin_lo, sin_hi, s)
    seq3 = lambda a: a.reshape(b, s, a.shape[-1])
    qk_bound = lambda gq, gk: (1.02 * math.sqrt(HEAD_DIM) * jnp.max(jnp.abs(gq))
                               * jnp.max(jnp.abs(gk))).reshape(1).astype(F32)
    bound_a = qk_bound(qn_a[l], kn_a[l]) + jnp.max(jnp.abs(rel_bias[l])).astype(F32)
    oa = _attn_a(bound_a, seq3(qa), seq3(ka), seq3(va), _clipped_table(rel_bias[l]))
    lam_vecs = jnp.stack([lambda_q1[l], lambda_k1[l], lambda_q2[l], lambda_k2[l]]).astype(F32)
    ob_lo, ob_hi = _attn_b(qk_bound(qn_b[l], kn_b[l]), seq3(qb), seq3(kb), seq3(vb), lam_vecs,
                           g_subln[l][None])
    out = _post(x1, oa.reshape(t, A_WIDTH), ob_lo.reshape(t // 2, B_WIDTH),
                ob_hi.reshape(t // 2, B_WIDTH), gates, w_up_a[l], w_up_b[l], wo,
                g_ffn2[l][None], wgu2, w_ffn2_down[l], g_final[l][None], s)
    return out.reshape(b, s, d)
```

```python
import functools
import math

import jax
import jax.numpy as jnp
import numpy as np
from jax import lax
from jax.experimental import pallas as pl
from jax.experimental.pallas import tpu as pltpu

F32 = jnp.float32
BF16 = jnp.bfloat16

D_MODEL = 1024
D_FF = 2816
CHUNK = 64
CHUNK_SHIFT = CHUNK.bit_length() - 1
assert 1 << CHUNK_SHIFT == CHUNK
LEFT_CHUNKS = 8
HEAD_DIM = 64
A_HEADS = 8
A_WIDTH = A_HEADS * HEAD_DIM
REL_CLIP = 256
B_HEADS = 4
B_WIDTH = B_HEADS * 2 * HEAD_DIM
ROPE_THETA = 500000.0
ROT_DIM = HEAD_DIM // 4
EPS = 1e-6
NEG = -1e30
LAMBDA_INIT = 0.8 - 0.6 * math.exp(-0.3 * 0)

LANES = 128
CAST_ROWS = 16
TOKEN_TILE = 512
FF_TILE = 256
A_Q_TILE = 256
A_BAND = A_Q_TILE + LEFT_CHUNKS * CHUNK
A_STEP_TILES = 4
A_STRIP_ROWS = 128
A_BIAS_W = 2 * LEFT_CHUNKS * CHUNK + A_BAND
A_TABLE_W = 2048
assert A_TABLE_W >= A_BIAS_W + A_STRIP_ROWS - 1 and A_Q_TILE % A_STRIP_ROWS == 0
B_Q_TILE = 256
B_K_TILE = B_Q_TILE
B_STEP_PAIRS = 1
SHIFT_SAFE = 40.0
V7X_VMEM_BYTES = 64 * 1024 * 1024
VMEM_LIMIT = V7X_VMEM_BYTES - 8 * 1024 * 1024


def _resident(shape):
    return pl.BlockSpec(shape, lambda *_: (0,) * len(shape), pipeline_mode=pl.Buffered(1))


def _rms(x, g):
    ms = jnp.mean(x * x, axis=-1, keepdims=True)
    return x * lax.rsqrt(ms + EPS) * g


def _swiglu(xn, wgu_ref, wdn_ref, acc_ref):
    for c in range(D_FF // FF_TILE):
        lo = c * FF_TILE
        w_g = wgu_ref[:, lo:lo + FF_TILE].astype(BF16)
        w_u = wgu_ref[:, D_FF + lo:D_FF + lo + FF_TILE].astype(BF16)
        g = jnp.dot(xn, w_g, preferred_element_type=F32)
        u = jnp.dot(xn, w_u, preferred_element_type=F32)
        a = (g * jax.nn.sigmoid(g) * u).astype(BF16)
        d = jnp.dot(a, wdn_ref[lo:lo + FF_TILE, :].astype(BF16), preferred_element_type=F32)
        if c == 0:
            acc_ref[...] = d
        else:
            acc_ref[...] += d


def _ffn1_kernel(x_ref, g_ref, wgu_ref, wdn_ref, *refs):
    n_next = (len(refs) - 2) // 2
    o_ref, acc_ref = refs[n_next], refs[-1]
    x = x_ref[...]
    _swiglu(_rms(x, g_ref[...]).astype(BF16), wgu_ref, wdn_ref, acc_ref)
    o_ref[...] = x + 0.5 * acc_ref[...]
    for src, dst in zip(refs[:n_next], refs[n_next + 1:-1]):
        dst[...] = src[...].astype(BF16)


def _ffn1(x, g, wgu, wdn, w_next):
    t = x.shape[0]
    steps = t // TOKEN_TILE
    row = pl.BlockSpec((TOKEN_TILE, D_MODEL), lambda i: (i, 0))
    assert all(w.shape[0] % (steps * CAST_ROWS) == 0 for w in w_next)
    cast_specs = [pl.BlockSpec((w.shape[0] // steps, w.shape[1]), lambda i: (i, 0))
                  for w in w_next]
    return pl.pallas_call(
        _ffn1_kernel,
        grid=(steps,),
        in_specs=[row, _resident((1, D_MODEL)), _resident(wgu.shape), _resident(wdn.shape)]
        + cast_specs,
        out_specs=[row] + cast_specs,
        out_shape=[jax.ShapeDtypeStruct(x.shape, F32)]
        + [jax.ShapeDtypeStruct(w.shape, BF16) for w in w_next],
        scratch_shapes=[pltpu.VMEM((TOKEN_TILE, D_MODEL), F32)],
        compiler_params=pltpu.CompilerParams(
            dimension_semantics=("arbitrary",), vmem_limit_bytes=VMEM_LIMIT),
        name="ffn1",
    )(x, g, wgu, wdn, *w_next)


def _head_norm(x, g2):
    lane = lax.broadcasted_iota(jnp.int32, x.shape, 1)
    lo = lane < HEAD_DIM
    sq = x * x
    s_lo = jnp.sum(jnp.where(lo, sq, 0.0), axis=-1, keepdims=True)
    s_hi = jnp.sum(jnp.where(lo, 0.0, sq), axis=-1, keepdims=True)
    ms = jnp.where(lo, s_lo, s_hi) * (1.0 / HEAD_DIM)
    return x * lax.rsqrt(ms + EPS) * g2


def _rotary(x, cos, sin_lo, sin_hi):
    half = ROT_DIM // 2
    up = pltpu.roll(x, LANES - half, 1)
    dn = pltpu.roll(x, half, 1)
    return x * cos + up * sin_lo + dn * sin_hi


def _in_proj_kernel(x_ref, g_ref, w_ref, qna_ref, kna_ref, qnb_ref, knb_ref,
                    cos_ref, sl_ref, sh_ref,
                    qa_ref, ka_ref, va_ref, qb_ref, kb_ref, vb_ref, gate_ref):
    h = _rms(x_ref[...], g_ref[...]).astype(BF16)
    scale = 1.0 / math.sqrt(HEAD_DIM)

    def proj(col, width):
        return jnp.dot(h, w_ref[:, col:col + width].astype(BF16), preferred_element_type=F32)

    def normed(col, gain_ref, out_ref, rotate, mult):
        y = proj(col, A_WIDTH)
        for p in range(A_WIDTH // LANES):
            t = _head_norm(y[:, p * LANES:(p + 1) * LANES], gain_ref[...])
            if rotate:
                t = _rotary(t, cos_ref[...], sl_ref[...], sh_ref[...])
            if mult != 1.0:
                t = t * mult
            out_ref[:, p * LANES:(p + 1) * LANES] = t.astype(BF16)

    normed(0 * A_WIDTH, qna_ref, qa_ref, False, scale)
    normed(1 * A_WIDTH, kna_ref, ka_ref, False, 1.0)
    va_ref[...] = proj(2 * A_WIDTH, A_WIDTH).astype(BF16)
    normed(3 * A_WIDTH, qnb_ref, qb_ref, True, scale)
    normed(4 * A_WIDTH, knb_ref, kb_ref, True, 1.0)
    vb_ref[...] = proj(5 * A_WIDTH, B_WIDTH).astype(BF16)
    for c in range(2 * D_MODEL // A_WIDTH):
        gate_ref[:, c * A_WIDTH:(c + 1) * A_WIDTH] = proj(6 * A_WIDTH + c * A_WIDTH, A_WIDTH)


def _in_proj(x1, g, w_in, qna, kna, qnb, knb, cos, sin_lo, sin_hi, seq):
    t = x1.shape[0]
    tiles_per_seq = seq // TOKEN_TILE
    row = lambda w: pl.BlockSpec((TOKEN_TILE, w), lambda i: (i, 0))
    pos = pl.BlockSpec((TOKEN_TILE, LANES), lambda i: (i % tiles_per_seq, 0))
    gain = _resident((1, LANES))
    half = jax.ShapeDtypeStruct((t, A_WIDTH), BF16)
    return pl.pallas_call(
        _in_proj_kernel,
        grid=(t // TOKEN_TILE,),
        in_specs=[row(D_MODEL), _resident((1, D_MODEL)), _resident(w_in.shape),
                  gain, gain, gain, gain, pos, pos, pos],
        out_specs=[row(A_WIDTH)] * 6 + [row(2 * D_MODEL)],
        out_shape=[half] * 6 + [jax.ShapeDtypeStruct((t, 2 * D_MODEL), F32)],
        compiler_params=pltpu.CompilerParams(
            dimension_semantics=("arbitrary",), vmem_limit_bytes=VMEM_LIMIT),
        name="in_proj",
    )(x1, g, w_in, qna, kna, qnb, knb, cos, sin_lo, sin_hi)


def _fill_band_bias(f_ref, bias_ref, shift):
    pad = LEFT_CHUNKS * CHUNK
    shape = (A_STRIP_ROWS, A_BIAS_W)
    q_chunk = lax.broadcasted_iota(jnp.int32, shape, 0) >> CHUNK_SHIFT
    k_chunk = (lax.broadcasted_iota(jnp.int32, shape, 1) - 2 * pad) >> CHUNK_SHIFT
    valid = jnp.logical_and(k_chunk <= q_chunk, k_chunk >= q_chunk - LEFT_CHUNKS)
    for h in range(A_HEADS):
        row = jnp.broadcast_to(f_ref[h:h + 1, :], (A_STRIP_ROWS, A_TABLE_W))
        skew = pltpu.roll(row, 0, 1, stride=1, stride_axis=0)
        bias_ref[h] = jnp.where(valid, skew[:, :A_BIAS_W] - shift, NEG)


def _attn_a_kernel(bound_ref, q_ref, k_ref, v_ref, f_ref, o_ref, bias_ref):
    i = pl.program_id(1)
    pad = LEFT_CHUNKS * CHUNK
    bound = bound_ref[0]
    shift_by_bound = bound <= SHIFT_SAFE

    @pl.when(jnp.logical_and(pl.program_id(0) == 0, i == 0))
    def _():
        _fill_band_bias(f_ref, bias_ref, jnp.where(shift_by_bound, bound, 0.0))

    lane = lax.broadcasted_iota(jnp.int32, (A_Q_TILE, LANES), 1)
    first = lane < HEAD_DIM

    def tile(shifted, sub):
        tile_start = (i * A_STEP_TILES + sub) * A_Q_TILE
        rows = slice(sub * A_Q_TILE, (sub + 1) * A_Q_TILE)
        start = pl.multiple_of(jnp.maximum(tile_start - pad, 0), A_Q_TILE)
        off = 2 * pad - tile_start + start
        for p in range(A_WIDTH // LANES):
            cols = slice(p * LANES, (p + 1) * LANES)
            q = q_ref[0, rows, cols]
            zero = jnp.zeros_like(q)
            qs = jnp.concatenate([jnp.where(first, q, zero), jnp.where(first, zero, q)], axis=0)
            k = k_ref[0, pl.ds(start, A_BAND), cols]
            v = v_ref[0, pl.ds(start, A_BAND), cols]
            s = lax.dot_general(qs, k, (((1,), (1,)), ((), ())), preferred_element_type=F32)
            bias = [bias_ref[2 * p + e, :,
                             pl.ds(pl.multiple_of(off - t * A_STRIP_ROWS, LANES), A_BAND)]
                    for e in range(2) for t in range(A_Q_TILE // A_STRIP_ROWS)]
            s = s + jnp.concatenate(bias, axis=0)
            if not shifted:
                s = s - jnp.max(s, axis=-1, keepdims=True)
            pr = jnp.exp(s)
            l = jnp.sum(pr, axis=-1, keepdims=True)
            o = jnp.dot(pr.astype(BF16), v, preferred_element_type=F32) / l
            o_ref[0, rows, cols] = jnp.where(first, o[:A_Q_TILE], o[A_Q_TILE:]).astype(BF16)

    def step(shifted):
        for sub in range(A_STEP_TILES):
            tile(shifted, sub)

    pl.when(shift_by_bound)(functools.partial(step, True))
    pl.when(jnp.logical_not(shift_by_bound))(functools.partial(step, False))


def _attn_a(bound, q, k, v, f_table):
    b, s, _ = q.shape
    whole = pl.BlockSpec((1, s, A_WIDTH), lambda bi, i: (bi, 0, 0))
    step_rows = A_STEP_TILES * A_Q_TILE
    tile = pl.BlockSpec((1, step_rows, A_WIDTH), lambda bi, i: (bi, i, 0))
    return pl.pallas_call(
        _attn_a_kernel,
        grid=(b, s // step_rows),
        in_specs=[pl.BlockSpec(memory_space=pltpu.SMEM), tile, whole, whole,
                  _resident(f_table.shape)],
        out_specs=tile,
        out_shape=jax.ShapeDtypeStruct(q.shape, BF16),
        scratch_shapes=[pltpu.VMEM((A_HEADS, A_STRIP_ROWS, A_BIAS_W), F32)],
        compiler_params=pltpu.CompilerParams(
            dimension_semantics=("arbitrary", "arbitrary"), vmem_limit_bytes=VMEM_LIMIT),
        name="attn_a",
    )(bound, q, k, v, f_table)


def _clipped_table(rel_table):
    pad = LEFT_CHUNKS * CHUNK
    first, last = rel_table[:1], rel_table[-1:]
    lead = 2 * pad - REL_CLIP
    tail = A_STRIP_ROWS - 1
    mid = A_TABLE_W - lead - rel_table.shape[0] - tail
    f = jnp.concatenate([jnp.repeat(last, lead, axis=0), rel_table[::-1],
                         jnp.repeat(first, mid, axis=0), jnp.repeat(last, tail, axis=0)], axis=0)
    return f.T.astype(F32)


def _attn_b_kernel(bound_ref, q0_ref, q1_ref, k_ref, v_ref, lam_ref, gs_ref, o0_ref, o1_ref,
                   qs_ref, m_ref, l_ref, acc_ref):
    i = pl.program_id(1)
    n_tiles = k_ref.shape[1] // B_Q_TILE
    n_full = n_tiles - 1
    nn = (((1,), (1,)), ((), ()))
    lane = lax.broadcasted_iota(jnp.int32, (B_Q_TILE, LANES), 1)
    first = lane < HEAD_DIM

    def tile_rows(pair, slot):
        pos = pair if slot == 0 else B_STEP_PAIRS - 1 - pair
        return slice(pos * B_Q_TILE, (pos + 1) * B_Q_TILE)

    def stack_queries(pair):
        for slot, q_ref in enumerate((q0_ref, q1_ref)):
            for h in range(B_HEADS):
                q = q_ref[0, tile_rows(pair, slot), h * LANES:(h + 1) * LANES]
                zero = jnp.zeros_like(q)
                qs_ref[2 * pair + slot, h, :B_Q_TILE] = jnp.where(first, q, zero)
                qs_ref[2 * pair + slot, h, B_Q_TILE:] = jnp.where(first, zero, q)

    def scores(slot, h, lo, width, diag_col):
        k = k_ref[0, pl.ds(lo, width), h * LANES:(h + 1) * LANES]
        s = lax.dot_general(qs_ref[slot, h], k, nn, preferred_element_type=F32)
        if diag_col is not None:
            row = lax.broadcasted_iota(jnp.int32, s.shape, 0) & (B_Q_TILE - 1)
            col = lax.broadcasted_iota(jnp.int32, s.shape, 1) - diag_col
            s = jnp.where((col >> CHUNK_SHIFT) <= (row >> CHUNK_SHIFT), s, NEG)
        return [s[:, c * LANES:(c + 1) * LANES] for c in range(width // LANES)]

    all_heads = tuple(range(B_HEADS))

    def max_block(slot, lo, width, diag_col, assign, heads=all_heads):
        for h in heads:
            m = functools.reduce(jnp.maximum, scores(slot, h, lo, width, diag_col))
            m_ref[slot, h] = m if assign else jnp.maximum(m_ref[slot, h], m)

    def sum_block(shift, slot, lo, width, diag_col, assign, heads=all_heads):
        for h in heads:
            m = m_ref[slot, h] if shift is None else shift
            ps = [jnp.exp(s - m) for s in scores(slot, h, lo, width, diag_col)]
            v = v_ref[0, pl.ds(lo, width), h * LANES:(h + 1) * LANES]
            l = functools.reduce(jnp.add, ps)
            pv = jnp.dot(jnp.concatenate(ps, axis=1).astype(BF16), v, preferred_element_type=F32)
            l_ref[slot, h] = l if assign else l_ref[slot, h] + l
            acc_ref[slot, h] = pv if assign else acc_ref[slot, h] + pv

    def sweep(block, unroll, pair, slot_done=None):
        a = i * B_STEP_PAIRS + pair
        for slot, tile in enumerate((a, n_tiles - 1 - a)):
            block(2 * pair + slot, pl.multiple_of(tile * B_Q_TILE, B_Q_TILE), B_Q_TILE, 0, True)

        def full(t):
            in_slot0 = t < a
            lo = jnp.where(in_slot0, t, t - a) * B_K_TILE
            block(2 * pair + jnp.where(in_slot0, 0, 1), pl.multiple_of(lo, B_K_TILE), B_K_TILE,
                  None, False)

        if unroll:
            shared = n_tiles // 2 - 1
            for t in range(shared):
                full(t)
            slot_done(0, all_heads)
            for h in all_heads:
                for t in range(shared, n_full):
                    block(2 * pair + 1, pl.multiple_of((t - a) * B_K_TILE, B_K_TILE), B_K_TILE,
                          None, False, heads=(h,))
                slot_done(1, (h,))
        else:
            def body(t, carry):
                full(t)
                return carry
            lax.fori_loop(0, n_full, body, 0)

    bound = bound_ref[0]
    shift_by_bound = bound <= SHIFT_SAFE

    def finish(pair, slot, heads=all_heads):
        lq1, lk1, lq2, lk2 = (lam_ref[n:n + 1, :] for n in range(4))
        lam = (jnp.exp(jnp.sum(lq1 * lk1, axis=-1, keepdims=True))
               - jnp.exp(jnp.sum(lq2 * lk2, axis=-1, keepdims=True)) + LAMBDA_INIT)
        o_ref = (o0_ref, o1_ref)[slot]
        idx = 2 * pair + slot
        for h in heads:
            o = acc_ref[idx, h] / jnp.sum(l_ref[idx, h], axis=-1, keepdims=True)
            o = o[:B_Q_TILE] - lam * o[B_Q_TILE:]
            o_ref[0, tile_rows(pair, slot), h * LANES:(h + 1) * LANES] = (
                _rms(o, gs_ref[...]) * (1.0 - LAMBDA_INIT)).astype(BF16)

    @pl.when(shift_by_bound)
    def _():
        for pair in range(B_STEP_PAIRS):
            stack_queries(pair)
            sweep(functools.partial(sum_block, bound), True, pair,
                  slot_done=functools.partial(finish, pair))

    @pl.when(jnp.logical_not(shift_by_bound))
    def _():
        for pair in range(B_STEP_PAIRS):
            stack_queries(pair)
            sweep(max_block, False, pair)
            for idx in (2 * pair, 2 * pair + 1):
                for h in range(B_HEADS):
                    m_ref[idx, h] = jnp.broadcast_to(
                        jnp.max(m_ref[idx, h], axis=-1, keepdims=True), m_ref.shape[2:])
            sweep(functools.partial(sum_block, None), False, pair)
            finish(pair, 0)
            finish(pair, 1)


def _attn_b(bound, q, k, v, lam_vecs, g_sub):
    b, s, _ = q.shape
    n_tiles = s // B_Q_TILE
    assert n_tiles % (2 * B_STEP_PAIRS) == 0 and B_K_TILE == B_Q_TILE
    steps = n_tiles // (2 * B_STEP_PAIRS)
    rows = B_STEP_PAIRS * B_Q_TILE
    whole = pl.BlockSpec((1, s, B_WIDTH), lambda bi, i: (bi, 0, 0))
    q_lo = pl.BlockSpec((1, rows, B_WIDTH), lambda bi, i: (bi, i, 0))
    q_hi = pl.BlockSpec((1, rows, B_WIDTH), lambda bi, i: (bi, 2 * steps - 1 - i, 0))
    o_hi = pl.BlockSpec((1, rows, B_WIDTH), lambda bi, i: (bi, steps - 1 - i, 0))
    stat = pltpu.VMEM((2 * B_STEP_PAIRS, B_HEADS, 2 * B_Q_TILE, LANES), F32)
    out = jax.ShapeDtypeStruct((b, s // 2, B_WIDTH), BF16)
    return pl.pallas_call(
        _attn_b_kernel,
        grid=(b, steps),
        in_specs=[pl.BlockSpec(memory_space=pltpu.SMEM), q_lo, q_hi, whole, whole,
                  _resident(lam_vecs.shape), _resident(g_sub.shape)],
        out_specs=[q_lo, o_hi],
        out_shape=[out, out],
        scratch_shapes=[pltpu.VMEM((2 * B_STEP_PAIRS, B_HEADS, 2 * B_Q_TILE, LANES), BF16),
                        stat, stat, stat],
        compiler_params=pltpu.CompilerParams(
            dimension_semantics=("arbitrary", "arbitrary"), vmem_limit_bytes=VMEM_LIMIT),
        name="attn_b",
    )(bound, q, q, k, v, lam_vecs, g_sub)


def _post_kernel(x_ref, oa_ref, ob_lo_ref, ob_hi_ref, gate_ref, wa_ref, wb_ref, wo_ref,
                 g2_ref, wgu_ref, wdn_ref, gf_ref, o_ref, acc_ref, *, tiles_per_seq):
    ya = jnp.dot(oa_ref[...], wa_ref[...].astype(BF16), preferred_element_type=F32)
    in_lo = (pl.program_id(0) % tiles_per_seq) < tiles_per_seq // 2
    ob = jnp.where(in_lo, ob_lo_ref[...], ob_hi_ref[...])
    yb = jnp.dot(ob, wb_ref[...].astype(BF16), preferred_element_type=F32)
    y = (jax.nn.sigmoid(gate_ref[:, :D_MODEL]) * ya
         + jax.nn.sigmoid(gate_ref[:, D_MODEL:]) * yb)
    x2 = x_ref[...] + jnp.dot(y.astype(BF16), wo_ref[...], preferred_element_type=F32)
    _swiglu(_rms(x2, g2_ref[...]).astype(BF16), wgu_ref, wdn_ref, acc_ref)
    o_ref[...] = _rms(x2 + 0.5 * acc_ref[...], gf_ref[...])


def _post(x1, oa, ob_lo, ob_hi, gates, wa, wb, wo, g2, wgu, wdn, gf, seq):
    t = x1.shape[0]
    tiles_per_seq = seq // TOKEN_TILE
    half = tiles_per_seq // 2
    assert tiles_per_seq % 2 == 0
    row = lambda w: pl.BlockSpec((TOKEN_TILE, w), lambda i: (i, 0))
    lo = pl.BlockSpec((TOKEN_TILE, B_WIDTH), lambda i: (
        (i // tiles_per_seq) * half + jnp.minimum(i % tiles_per_seq, half - 1), 0))
    hi = pl.BlockSpec((TOKEN_TILE, B_WIDTH), lambda i: (
        (i // tiles_per_seq) * half + jnp.maximum(i % tiles_per_seq - half, 0), 0))
    vec = _resident((1, D_MODEL))
    return pl.pallas_call(
        functools.partial(_post_kernel, tiles_per_seq=tiles_per_seq),
        grid=(t // TOKEN_TILE,),
        in_specs=[row(D_MODEL), row(A_WIDTH), lo, hi, row(2 * D_MODEL),
                  _resident(wa.shape), _resident(wb.shape), _resident(wo.shape),
                  vec, _resident(wgu.shape), _resident(wdn.shape), vec],
        out_specs=row(D_MODEL),
        out_shape=jax.ShapeDtypeStruct(x1.shape, F32),
        scratch_shapes=[pltpu.VMEM((TOKEN_TILE, D_MODEL), F32)],
        compiler_params=pltpu.CompilerParams(
            dimension_semantics=("arbitrary",), vmem_limit_bytes=VMEM_LIMIT),
        name="post",
    )(x1, oa, ob_lo, ob_hi, gates, wa, wb, wo, g2, wgu, wdn, gf)


def _rotary_tables(seq):
    half = ROT_DIM // 2
    pos = np.arange(seq, dtype=np.float64)
    inv = ROPE_THETA ** (-np.arange(0, ROT_DIM, 2, dtype=np.float64) / ROT_DIM)
    ang = pos[:, None] * inv[None, :]
    cos, sin = np.cos(ang), np.sin(ang)
    ones = np.ones((seq, HEAD_DIM - ROT_DIM))
    zeros = np.zeros((seq, HEAD_DIM - half))
    cos_h = np.concatenate([cos, cos, ones], axis=-1)
    sin_lo = np.concatenate([-sin, zeros], axis=-1)
    sin_hi = np.concatenate([np.zeros((seq, half)), sin,
                             np.zeros((seq, HEAD_DIM - ROT_DIM))], axis=-1)
    two = lambda a: jnp.asarray(np.concatenate([a, a], axis=-1), dtype=F32)
    return two(cos_h), two(sin_lo), two(sin_hi)


def kernel(x, g_ffn1, w_ffn1_gu, w_ffn1_down, g_mix, w_in, qn_a, kn_a, rel_bias, qn_b, kn_b,
           lambda_q1, lambda_k1, lambda_q2, lambda_k2, g_subln, w_up_a, w_up_b, w_out, g_ffn2,
           w_ffn2_gu, w_ffn2_down, g_final):
    b, s, d = x.shape
    assert d == D_MODEL and s % TOKEN_TILE == 0 and g_ffn1.shape[0] == 1
    l = 0
    t = b * s
    two = lambda g: jnp.concatenate([g, g], axis=-1)[None, :]
    cos, sin_lo, sin_hi = _rotary_tables(s)

    x1, wo, wgu2 = _ffn1(x.reshape(t, d), g_ffn1[l][None], w_ffn1_gu[l], w_ffn1_down[l],
                         [w_out[l], w_ffn2_gu[l]])
    qa, ka, va, qb, kb, vb, gates = _in_proj(
        x1, g_mix[l][None], w_in[l], two(qn_a[l]), two(kn_a[l]), two(qn_b[l]),
        two(kn_b[l]), cos, sin_lo, sin_hi, s)
    seq3 = lambda a: a.reshape(b, s, a.shape[-1])
    qk_bound = lambda gq, gk: (1.02 * math.sqrt(HEAD_DIM) * jnp.max(jnp.abs(gq))
                               * jnp.max(jnp.abs(gk))).reshape(1).astype(F32)
    bound_a = qk_bound(qn_a[l], kn_a[l]) + jnp.max(jnp.abs(rel_bias[l])).astype(F32)
    oa = _attn_a(bound_a, seq3(qa), seq3(ka), seq3(va), _clipped_table(rel_bias[l]))
    lam_vecs = jnp.stack([lambda_q1[l], lambda_k1[l], lambda_q2[l], lambda_k2[l]]).astype(F32)
    ob_lo, ob_hi = _attn_b(qk_bound(qn_b[l], kn_b[l]), seq3(qb), seq3(kb), seq3(vb), lam_vecs,
                           g_subln[l][None])
    out = _post(x1, oa.reshape(t, A_WIDTH), ob_lo.reshape(t // 2, B_WIDTH),
                ob_hi.reshape(t // 2, B_WIDTH), gates, w_up_a[l], w_up_b[l], wo,
                g_ffn2[l][None], wgu2, w_ffn2_down[l], g_final[l][None], s)
    return out.reshape(b, s, d)
```

```python
import functools
import math

import jax
import jax.numpy as jnp
import numpy as np
from jax import lax
from jax.experimental import pallas as pl
from jax.experimental.pallas import tpu as pltpu

F32 = jnp.float32
BF16 = jnp.bfloat16

D_MODEL = 1024
D_FF = 2816
CHUNK = 64
CHUNK_SHIFT = CHUNK.bit_length() - 1
assert 1 << CHUNK_SHIFT == CHUNK
LEFT_CHUNKS = 8
HEAD_DIM = 64
A_HEADS = 8
A_WIDTH = A_HEADS * HEAD_DIM
REL_CLIP = 256
B_HEADS = 4
B_WIDTH = B_HEADS * 2 * HEAD_DIM
ROPE_THETA = 500000.0
ROT_DIM = HEAD_DIM // 4
EPS = 1e-6
NEG = -1e30
LAMBDA_INIT = 0.8 - 0.6 * math.exp(-0.3 * 0)

LANES = 128
CAST_ROWS = 16
TOKEN_TILE = 512
IN_PROJ_TILE = 1024
FF_TILE = 256
A_Q_TILE = 256
A_BAND = A_Q_TILE + LEFT_CHUNKS * CHUNK
A_STEP_TILES = 4
A_STRIP_ROWS = 128
A_BIAS_W = 2 * LEFT_CHUNKS * CHUNK + A_BAND
A_TABLE_W = 2048
assert A_TABLE_W >= A_BIAS_W + A_STRIP_ROWS - 1 and A_Q_TILE % A_STRIP_ROWS == 0
B_Q_TILE = 256
B_K_TILE = B_Q_TILE
B_STEP_PAIRS = 1
SHIFT_SAFE = 40.0
V7X_VMEM_BYTES = 64 * 1024 * 1024
VMEM_LIMIT = V7X_VMEM_BYTES - 8 * 1024 * 1024


def _resident(shape):
    return pl.BlockSpec(shape, lambda *_: (0,) * len(shape), pipeline_mode=pl.Buffered(1))


def _rms(x, g):
    ms = jnp.mean(x * x, axis=-1, keepdims=True)
    return x * lax.rsqrt(ms + EPS) * g


def _swiglu(xn, wgu_ref, wdn_ref, acc_ref):
    for c in range(D_FF // FF_TILE):
        lo = c * FF_TILE
        w_g = wgu_ref[:, lo:lo + FF_TILE].astype(BF16)
        w_u = wgu_ref[:, D_FF + lo:D_FF + lo + FF_TILE].astype(BF16)
        g = jnp.dot(xn, w_g, preferred_element_type=F32)
        u = jnp.dot(xn, w_u, preferred_element_type=F32)
        a = (g * jax.nn.sigmoid(g) * u).astype(BF16)
        d = jnp.dot(a, wdn_ref[lo:lo + FF_TILE, :].astype(BF16), preferred_element_type=F32)
        if c == 0:
            acc_ref[...] = d
        else:
            acc_ref[...] += d


def _ffn1_kernel(x_ref, g_ref, wgu_ref, wdn_ref, *refs):
    n_next = (len(refs) - 2) // 2
    o_ref, acc_ref = refs[n_next], refs[-1]
    x = x_ref[...]
    _swiglu(_rms(x, g_ref[...]).astype(BF16), wgu_ref, wdn_ref, acc_ref)
    o_ref[...] = x + 0.5 * acc_ref[...]
    for src, dst in zip(refs[:n_next], refs[n_next + 1:-1]):
        dst[...] = src[...].astype(BF16)


def _ffn1(x, g, wgu, wdn, w_next):
    t = x.shape[0]
    steps = t // TOKEN_TILE
    row = pl.BlockSpec((TOKEN_TILE, D_MODEL), lambda i: (i, 0))
    assert all(w.shape[0] % (steps * CAST_ROWS) == 0 for w in w_next)
    cast_specs = [pl.BlockSpec((w.shape[0] // steps, w.shape[1]), lambda i: (i, 0))
                  for w in w_next]
    return pl.pallas_call(
        _ffn1_kernel,
        grid=(steps,),
        in_specs=[row, _resident((1, D_MODEL)), _resident(wgu.shape), _resident(wdn.shape)]
        + cast_specs,
        out_specs=[row] + cast_specs,
        out_shape=[jax.ShapeDtypeStruct(x.shape, F32)]
        + [jax.ShapeDtypeStruct(w.shape, BF16) for w in w_next],
        scratch_shapes=[pltpu.VMEM((TOKEN_TILE, D_MODEL), F32)],
        compiler_params=pltpu.CompilerParams(
            dimension_semantics=("arbitrary",), vmem_limit_bytes=VMEM_LIMIT),
        name="ffn1",
    )(x, g, wgu, wdn, *w_next)


def _head_norm(x, g2):
    lane = lax.broadcasted_iota(jnp.int32, x.shape, 1)
    lo = lane < HEAD_DIM
    sq = x * x
    s_lo = jnp.sum(jnp.where(lo, sq, 0.0), axis=-1, keepdims=True)
    s_hi = jnp.sum(jnp.where(lo, 0.0, sq), axis=-1, keepdims=True)
    ms = jnp.where(lo, s_lo, s_hi) * (1.0 / HEAD_DIM)
    return x * lax.rsqrt(ms + EPS) * g2


def _rotary(x, cos, sin_lo, sin_hi):
    half = ROT_DIM // 2
    up = pltpu.roll(x, LANES - half, 1)
    dn = pltpu.roll(x, half, 1)
    return x * cos + up * sin_lo + dn * sin_hi


def _in_proj_kernel(x_ref, g_ref, w_ref, qna_ref, kna_ref, qnb_ref, knb_ref,
                    cos_ref, sl_ref, sh_ref,
                    qa_ref, ka_ref, va_ref, qb_ref, kb_ref, vb_ref, gate_ref):
    h = _rms(x_ref[...], g_ref[...]).astype(BF16)
    scale = 1.0 / math.sqrt(HEAD_DIM)

    def proj(col, width):
        return jnp.dot(h, w_ref[:, col:col + width].astype(BF16), preferred_element_type=F32)

    def normed(col, gain_ref, out_ref, rotate, mult):
        y = proj(col, A_WIDTH)
        for p in range(A_WIDTH // LANES):
            t = _head_norm(y[:, p * LANES:(p + 1) * LANES], gain_ref[...])
            if rotate:
                t = _rotary(t, cos_ref[...], sl_ref[...], sh_ref[...])
            if mult != 1.0:
                t = t * mult
            out_ref[:, p * LANES:(p + 1) * LANES] = t.astype(BF16)

    normed(0 * A_WIDTH, qna_ref, qa_ref, False, scale)
    normed(1 * A_WIDTH, kna_ref, ka_ref, False, 1.0)
    va_ref[...] = proj(2 * A_WIDTH, A_WIDTH).astype(BF16)
    normed(3 * A_WIDTH, qnb_ref, qb_ref, True, scale)
    normed(4 * A_WIDTH, knb_ref, kb_ref, True, 1.0)
    vb_ref[...] = proj(5 * A_WIDTH, B_WIDTH).astype(BF16)
    for c in range(2 * D_MODEL // A_WIDTH):
        gate_ref[:, c * A_WIDTH:(c + 1) * A_WIDTH] = proj(6 * A_WIDTH + c * A_WIDTH, A_WIDTH)


def _in_proj(x1, g, w_in, qna, kna, qnb, knb, cos, sin_lo, sin_hi, seq):
    t = x1.shape[0]
    tiles_per_seq = seq // IN_PROJ_TILE
    row = lambda w: pl.BlockSpec((IN_PROJ_TILE, w), lambda i: (i, 0))
    pos = pl.BlockSpec((IN_PROJ_TILE, LANES), lambda i: (i % tiles_per_seq, 0))
    gain = _resident((1, LANES))
    half = jax.ShapeDtypeStruct((t, A_WIDTH), BF16)
    return pl.pallas_call(
        _in_proj_kernel,
        grid=(t // IN_PROJ_TILE,),
        in_specs=[row(D_MODEL), _resident((1, D_MODEL)), _resident(w_in.shape),
                  gain, gain, gain, gain, pos, pos, pos],
        out_specs=[row(A_WIDTH)] * 6 + [row(2 * D_MODEL)],
        out_shape=[half] * 6 + [jax.ShapeDtypeStruct((t, 2 * D_MODEL), F32)],
        compiler_params=pltpu.CompilerParams(
            dimension_semantics=("arbitrary",), vmem_limit_bytes=VMEM_LIMIT),
        name="in_proj",
    )(x1, g, w_in, qna, kna, qnb, knb, cos, sin_lo, sin_hi)


def _fill_band_bias(f_ref, bias_ref, shift):
    pad = LEFT_CHUNKS * CHUNK
    shape = (A_STRIP_ROWS, A_BIAS_W)
    q_chunk = lax.broadcasted_iota(jnp.int32, shape, 0) >> CHUNK_SHIFT
    k_chunk = (lax.broadcasted_iota(jnp.int32, shape, 1) - 2 * pad) >> CHUNK_SHIFT
    valid = jnp.logical_and(k_chunk <= q_chunk, k_chunk >= q_chunk - LEFT_CHUNKS)
    for h in range(A_HEADS):
        row = jnp.broadcast_to(f_ref[h:h + 1, :], (A_STRIP_ROWS, A_TABLE_W))
        skew = pltpu.roll(row, 0, 1, stride=1, stride_axis=0)
        bias_ref[h] = jnp.where(valid, skew[:, :A_BIAS_W] - shift, NEG)


def _attn_a_kernel(bound_ref, q_ref, k_ref, v_ref, f_ref, o_ref, bias_ref):
    i = pl.program_id(1)
    pad = LEFT_CHUNKS * CHUNK
    bound = bound_ref[0]
    shift_by_bound = bound <= SHIFT_SAFE

    @pl.when(jnp.logical_and(pl.program_id(0) == 0, i == 0))
    def _():
        _fill_band_bias(f_ref, bias_ref, jnp.where(shift_by_bound, bound, 0.0))

    lane = lax.broadcasted_iota(jnp.int32, (A_Q_TILE, LANES), 1)
    first = lane < HEAD_DIM

    def tile(shifted, sub):
        tile_start = (i * A_STEP_TILES + sub) * A_Q_TILE
        rows = slice(sub * A_Q_TILE, (sub + 1) * A_Q_TILE)
        start = pl.multiple_of(jnp.maximum(tile_start - pad, 0), A_Q_TILE)
        off = 2 * pad - tile_start + start
        for p in range(A_WIDTH // LANES):
            cols = slice(p * LANES, (p + 1) * LANES)
            q = q_ref[0, rows, cols]
            zero = jnp.zeros_like(q)
            qs = jnp.concatenate([jnp.where(first, q, zero), jnp.where(first, zero, q)], axis=0)
            k = k_ref[0, pl.ds(start, A_BAND), cols]
            v = v_ref[0, pl.ds(start, A_BAND), cols]
            s = lax.dot_general(qs, k, (((1,), (1,)), ((), ())), preferred_element_type=F32)
            bias = [bias_ref[2 * p + e, :,
                             pl.ds(pl.multiple_of(off - t * A_STRIP_ROWS, LANES), A_BAND)]
                    for e in range(2) for t in range(A_Q_TILE // A_STRIP_ROWS)]
            s = s + jnp.concatenate(bias, axis=0)
            if not shifted:
                s = s - jnp.max(s, axis=-1, keepdims=True)
            pr = jnp.exp(s)
            l = jnp.sum(pr, axis=-1, keepdims=True)
            o = jnp.dot(pr.astype(BF16), v, preferred_element_type=F32) / l
            o_ref[0, rows, cols] = jnp.where(first, o[:A_Q_TILE], o[A_Q_TILE:]).astype(BF16)

    def step(shifted):
        for sub in range(A_STEP_TILES):
            tile(shifted, sub)

    pl.when(shift_by_bound)(functools.partial(step, True))
    pl.when(jnp.logical_not(shift_by_bound))(functools.partial(step, False))


def _attn_a(bound, q, k, v, f_table):
    b, s, _ = q.shape
    whole = pl.BlockSpec((1, s, A_WIDTH), lambda bi, i: (bi, 0, 0))
    step_rows = A_STEP_TILES * A_Q_TILE
    tile = pl.BlockSpec((1, step_rows, A_WIDTH), lambda bi, i: (bi, i, 0))
    return pl.pallas_call(
        _attn_a_kernel,
        grid=(b, s // step_rows),
        in_specs=[pl.BlockSpec(memory_space=pltpu.SMEM), tile, whole, whole,
                  _resident(f_table.shape)],
        out_specs=tile,
        out_shape=jax.ShapeDtypeStruct(q.shape, BF16),
        scratch_shapes=[pltpu.VMEM((A_HEADS, A_STRIP_ROWS, A_BIAS_W), F32)],
        compiler_params=pltpu.CompilerParams(
            dimension_semantics=("arbitrary", "arbitrary"), vmem_limit_bytes=VMEM_LIMIT),
        name="attn_a",
    )(bound, q, k, v, f_table)


def _clipped_table(rel_table):
    pad = LEFT_CHUNKS * CHUNK
    first, last = rel_table[:1], rel_table[-1:]
    lead = 2 * pad - REL_CLIP
    tail = A_STRIP_ROWS - 1
    mid = A_TABLE_W - lead - rel_table.shape[0] - tail
    f = jnp.concatenate([jnp.repeat(last, lead, axis=0), rel_table[::-1],
                         jnp.repeat(first, mid, axis=0), jnp.repeat(last, tail, axis=0)], axis=0)
    return f.T.astype(F32)


def _attn_b_kernel(bound_ref, q0_ref, q1_ref, k_ref, v_ref, lam_ref, gs_ref, o0_ref, o1_ref,
                   qs_ref, m_ref, l_ref, acc_ref):
    i = pl.program_id(1)
    n_tiles = k_ref.shape[1] // B_Q_TILE
    n_full = n_tiles - 1
    nn = (((1,), (1,)), ((), ()))
    lane = lax.broadcasted_iota(jnp.int32, (B_Q_TILE, LANES), 1)
    first = lane < HEAD_DIM

    def tile_rows(pair, slot):
        pos = pair if slot == 0 else B_STEP_PAIRS - 1 - pair
        return slice(pos * B_Q_TILE, (pos + 1) * B_Q_TILE)

    def stack_queries(pair):
        for slot, q_ref in enumerate((q0_ref, q1_ref)):
            for h in range(B_HEADS):
                q = q_ref[0, tile_rows(pair, slot), h * LANES:(h + 1) * LANES]
                zero = jnp.zeros_like(q)
                qs_ref[2 * pair + slot, h, :B_Q_TILE] = jnp.where(first, q, zero)
                qs_ref[2 * pair + slot, h, B_Q_TILE:] = jnp.where(first, zero, q)

    def scores(slot, h, lo, width, diag_col):
        k = k_ref[0, pl.ds(lo, width), h * LANES:(h + 1) * LANES]
        s = lax.dot_general(qs_ref[slot, h], k, nn, preferred_element_type=F32)
        if diag_col is not None:
            row = lax.broadcasted_iota(jnp.int32, s.shape, 0) & (B_Q_TILE - 1)
            col = lax.broadcasted_iota(jnp.int32, s.shape, 1) - diag_col
            s = jnp.where((col >> CHUNK_SHIFT) <= (row >> CHUNK_SHIFT), s, NEG)
        return [s[:, c * LANES:(c + 1) * LANES] for c in range(width // LANES)]

    all_heads = tuple(range(B_HEADS))

    def max_block(slot, lo, width, diag_col, assign, heads=all_heads):
        for h in heads:
            m = functools.reduce(jnp.maximum, scores(slot, h, lo, width, diag_col))
            m_ref[slot, h] = m if assign else jnp.maximum(m_ref[slot, h], m)

    def sum_block(shift, slot, lo, width, diag_col, assign, heads=all_heads):
        for h in heads:
            m = m_ref[slot, h] if shift is None else shift
            ps = [jnp.exp(s - m) for s in scores(slot, h, lo, width, diag_col)]
            v = v_ref[0, pl.ds(lo, width), h * LANES:(h + 1) * LANES]
            l = functools.reduce(jnp.add, ps)
            pv = jnp.dot(jnp.concatenate(ps, axis=1).astype(BF16), v, preferred_element_type=F32)
            l_ref[slot, h] = l if assign else l_ref[slot, h] + l
            acc_ref[slot, h] = pv if assign else acc_ref[slot, h] + pv

    def sweep(block, unroll, pair, slot_done=None):
        a = i * B_STEP_PAIRS + pair
        for slot, tile in enumerate((a, n_tiles - 1 - a)):
            block(2 * pair + slot, pl.multiple_of(tile * B_Q_TILE, B_Q_TILE), B_Q_TILE, 0, True)

        def full(t):
            in_slot0 = t < a
            lo = jnp.where(in_slot0, t, t - a) * B_K_TILE
            block(2 * pair + jnp.where(in_slot0, 0, 1), pl.multiple_of(lo, B_K_TILE), B_K_TILE,
                  None, False)

        if unroll:
            shared = n_tiles // 2 - 1
            for t in range(shared):
                full(t)
            slot_done(0, all_heads)
            for h in all_heads:
                for t in range(shared, n_full):
                    block(2 * pair + 1, pl.multiple_of((t - a) * B_K_TILE, B_K_TILE), B_K_TILE,
                          None, False, heads=(h,))
                slot_done(1, (h,))
        else:
            def body(t, carry):
                full(t)
                return carry
            lax.fori_loop(0, n_full, body, 0)

    bound = bound_ref[0]
    shift_by_bound = bound <= SHIFT_SAFE

    def finish(pair, slot, heads=all_heads):
        lq1, lk1, lq2, lk2 = (lam_ref[n:n + 1, :] for n in range(4))
        lam = (jnp.exp(jnp.sum(lq1 * lk1, axis=-1, keepdims=True))
               - jnp.exp(jnp.sum(lq2 * lk2, axis=-1, keepdims=True)) + LAMBDA_INIT)
        o_ref = (o0_ref, o1_ref)[slot]
        idx = 2 * pair + slot
        for h in heads:
            o = acc_ref[idx, h] / jnp.sum(l_ref[idx, h], axis=-1, keepdims=True)
            o = o[:B_Q_TILE] - lam * o[B_Q_TILE:]
            o_ref[0, tile_rows(pair, slot), h * LANES:(h + 1) * LANES] = (
                _rms(o, gs_ref[...]) * (1.0 - LAMBDA_INIT)).astype(BF16)

    @pl.when(shift_by_bound)
    def _():
        for pair in range(B_STEP_PAIRS):
            stack_queries(pair)
            sweep(functools.partial(sum_block, bound), True, pair,
                  slot_done=functools.partial(finish, pair))

    @pl.when(jnp.logical_not(shift_by_bound))
    def _():
        for pair in range(B_STEP_PAIRS):
            stack_queries(pair)
            sweep(max_block, False, pair)
            for idx in (2 * pair, 2 * pair + 1):
                for h in range(B_HEADS):
                    m_ref[idx, h] = jnp.broadcast_to(
                        jnp.max(m_ref[idx, h], axis=-1, keepdims=True), m_ref.shape[2:])
            sweep(functools.partial(sum_block, None), False, pair)
            finish(pair, 0)
            finish(pair, 1)


def _attn_b(bound, q, k, v, lam_vecs, g_sub):
    b, s, _ = q.shape
    n_tiles = s // B_Q_TILE
    assert n_tiles % (2 * B_STEP_PAIRS) == 0 and B_K_TILE == B_Q_TILE
    steps = n_tiles // (2 * B_STEP_PAIRS)
    rows = B_STEP_PAIRS * B_Q_TILE
    whole = pl.BlockSpec((1, s, B_WIDTH), lambda bi, i: (bi, 0, 0))
    q_lo = pl.BlockSpec((1, rows, B_WIDTH), lambda bi, i: (bi, i, 0))
    q_hi = pl.BlockSpec((1, rows, B_WIDTH), lambda bi, i: (bi, 2 * steps - 1 - i, 0))
    o_hi = pl.BlockSpec((1, rows, B_WIDTH), lambda bi, i: (bi, steps - 1 - i, 0))
    stat = pltpu.VMEM((2 * B_STEP_PAIRS, B_HEADS, 2 * B_Q_TILE, LANES), F32)
    out = jax.ShapeDtypeStruct((b, s // 2, B_WIDTH), BF16)
    return pl.pallas_call(
        _attn_b_kernel,
        grid=(b, steps),
        in_specs=[pl.BlockSpec(memory_space=pltpu.SMEM), q_lo, q_hi, whole, whole,
                  _resident(lam_vecs.shape), _resident(g_sub.shape)],
        out_specs=[q_lo, o_hi],
        out_shape=[out, out],
        scratch_shapes=[pltpu.VMEM((2 * B_STEP_PAIRS, B_HEADS, 2 * B_Q_TILE, LANES), BF16),
                        stat, stat, stat],
        compiler_params=pltpu.CompilerParams(
            dimension_semantics=("arbitrary", "arbitrary"), vmem_limit_bytes=VMEM_LIMIT),
        name="attn_b",
    )(bound, q, q, k, v, lam_vecs, g_sub)


def _post_kernel(x_ref, oa_ref, ob_lo_ref, ob_hi_ref, gate_ref, wa_ref, wb_ref, wo_ref,
                 g2_ref, wgu_ref, wdn_ref, gf_ref, o_ref, acc_ref, *, tiles_per_seq):
    ya = jnp.dot(oa_ref[...], wa_ref[...].astype(BF16), preferred_element_type=F32)
    in_lo = (pl.program_id(0) % tiles_per_seq) < tiles_per_seq // 2
    ob = jnp.where(in_lo, ob_lo_ref[...], ob_hi_ref[...])
    yb = jnp.dot(ob, wb_ref[...].astype(BF16), preferred_element_type=F32)
    y = (jax.nn.sigmoid(gate_ref[:, :D_MODEL]) * ya
         + jax.nn.sigmoid(gate_ref[:, D_MODEL:]) * yb)
    x2 = x_ref[...] + jnp.dot(y.astype(BF16), wo_ref[...], preferred_element_type=F32)
    _swiglu(_rms(x2, g2_ref[...]).astype(BF16), wgu_ref, wdn_ref, acc_ref)
    o_ref[...] = _rms(x2 + 0.5 * acc_ref[...], gf_ref[...])


def _post(x1, oa, ob_lo, ob_hi, gates, wa, wb, wo, g2, wgu, wdn, gf, seq):
    t = x1.shape[0]
    tiles_per_seq = seq // TOKEN_TILE
    half = tiles_per_seq // 2
    assert tiles_per_seq % 2 == 0
    row = lambda w: pl.BlockSpec((TOKEN_TILE, w), lambda i: (i, 0))
    lo = pl.BlockSpec((TOKEN_TILE, B_WIDTH), lambda i: (
        (i // tiles_per_seq) * half + jnp.minimum(i % tiles_per_seq, half - 1), 0))
    hi = pl.BlockSpec((TOKEN_TILE, B_WIDTH), lambda i: (
        (i // tiles_per_seq) * half + jnp.maximum(i % tiles_per_seq - half, 0), 0))
    vec = _resident((1, D_MODEL))
    return pl.pallas_call(
        functools.partial(_post_kernel, tiles_per_seq=tiles_per_seq),
        grid=(t // TOKEN_TILE,),
        in_specs=[row(D_MODEL), row(A_WIDTH), lo, hi, row(2 * D_MODEL),
                  _resident(wa.shape), _resident(wb.shape), _resident(wo.shape),
                  vec, _resident(wgu.shape), _resident(wdn.shape), vec],
        out_specs=row(D_MODEL),
        out_shape=jax.ShapeDtypeStruct(x1.shape, F32),
        scratch_shapes=[pltpu.VMEM((TOKEN_TILE, D_MODEL), F32)],
        compiler_params=pltpu.CompilerParams(
            dimension_semantics=("arbitrary",), vmem_limit_bytes=VMEM_LIMIT),
        name="post",
    )(x1, oa, ob_lo, ob_hi, gates, wa, wb, wo, g2, wgu, wdn, gf)


def _rotary_tables(seq):
    half = ROT_DIM // 2
    pos = np.arange(seq, dtype=np.float64)
    inv = ROPE_THETA ** (-np.arange(0, ROT_DIM, 2, dtype=np.float64) / ROT_DIM)
    ang = pos[:, None] * inv[None, :]
    cos, sin = np.cos(ang), np.sin(ang)
    ones = np.ones((seq, HEAD_DIM - ROT_DIM))
    zeros = np.zeros((seq, HEAD_DIM - half))
    cos_h = np.concatenate([cos, cos, ones], axis=-1)
    sin_lo = np.concatenate([-sin, zeros], axis=-1)
    sin_hi = np.concatenate([np.zeros((seq, half)), sin,
                             np.zeros((seq, HEAD_DIM - ROT_DIM))], axis=-1)
    two = lambda a: jnp.asarray(np.concatenate([a, a], axis=-1), dtype=F32)
    return two(cos_h), two(sin_lo), two(sin_hi)


def kernel(x, g_ffn1, w_ffn1_gu, w_ffn1_down, g_mix, w_in, qn_a, kn_a, rel_bias, qn_b, kn_b,
           lambda_q1, lambda_k1, lambda_q2, lambda_k2, g_subln, w_up_a, w_up_b, w_out, g_ffn2,
           w_ffn2_gu, w_ffn2_down, g_final):
    b, s, d = x.shape
    assert d == D_MODEL and s % TOKEN_TILE == 0 and g_ffn1.shape[0] == 1
    l = 0
    t = b * s
    two = lambda g: jnp.concatenate([g, g], axis=-1)[None, :]
    cos, sin_lo, sin_hi = _rotary_tables(s)

    x1, wo, wgu2, w_in_b = _ffn1(x.reshape(t, d), g_ffn1[l][None], w_ffn1_gu[l], w_ffn1_down[l],
                                 [w_out[l], w_ffn2_gu[l], w_in[l]])
    qa, ka, va, qb, kb, vb, gates = _in_proj(
        x1, g_mix[l][None], w_in_b, two(qn_a[l]), two(kn_a[l]), two(qn_b[l]),
        two(kn_b[l]), cos, sin_lo, sin_hi, s)
    seq3 = lambda a: a.reshape(b, s, a.shape[-1])
    qk_bound = lambda gq, gk: (1.02 * math.sqrt(HEAD_DIM) * jnp.max(jnp.abs(gq))
                               * jnp.max(jnp.abs(gk))).reshape(1).astype(F32)
    bound_a = qk_bound(qn_a[l], kn_a[l]) + jnp.max(jnp.abs(rel_bias[l])).astype(F32)
    oa = _attn_a(bound_a, seq3(qa), seq3(ka), seq3(va), _clipped_table(rel_bias[l]))
    lam_vecs = jnp.stack([lambda_q1[l], lambda_k1[l], lambda_q2[l], lambda_k2[l]]).astype(F32)
    ob_lo, ob_hi = _attn_b(qk_bound(qn_b[l], kn_b[l]), seq3(qb), seq3(kb), seq3(vb), lam_vecs,
                           g_subln[l][None])
    out = _post(x1, oa.reshape(t, A_WIDTH), ob_lo.reshape(t // 2, B_WIDTH),
                ob_hi.reshape(t // 2, B_WIDTH), gates, w_up_a[l], w_up_b[l], wo,
                g_ffn2[l][None], wgu2, w_ffn2_down[l], g_final[l][None], s)
    return out.reshape(b, s, d)
```

```python
import functools
import math

import jax
import jax.numpy as jnp
import numpy as np
from jax import lax
from jax.experimental import pallas as pl
from jax.experimental.pallas import tpu as pltpu

F32 = jnp.float32
BF16 = jnp.bfloat16

D_MODEL = 1024
D_FF = 2816
CHUNK = 64
CHUNK_SHIFT = CHUNK.bit_length() - 1
assert 1 << CHUNK_SHIFT == CHUNK
LEFT_CHUNKS = 8
HEAD_DIM = 64
A_HEADS = 8
A_WIDTH = A_HEADS * HEAD_DIM
REL_CLIP = 256
B_HEADS = 4
B_WIDTH = B_HEADS * 2 * HEAD_DIM
ROPE_THETA = 500000.0
ROT_DIM = HEAD_DIM // 4
EPS = 1e-6
NEG = -1e30
LOG2E = math.log2(math.e)
LAMBDA_INIT = 0.8 - 0.6 * math.exp(-0.3 * 0)

LANES = 128
CAST_ROWS = 16
TOKEN_TILE = 512
FF_TILE = 256
A_Q_TILE = 256
A_BAND = A_Q_TILE + LEFT_CHUNKS * CHUNK
A_STEP_TILES = 4
A_STRIP_ROWS = 128
A_BIAS_W = 2 * LEFT_CHUNKS * CHUNK + A_BAND
A_TABLE_W = 2048
assert A_TABLE_W >= A_BIAS_W + A_STRIP_ROWS - 1 and A_Q_TILE % A_STRIP_ROWS == 0
B_Q_TILE = 256
B_K_TILE = B_Q_TILE
B_STEP_PAIRS = 1
SHIFT_SAFE = 40.0
V7X_VMEM_BYTES = 64 * 1024 * 1024
VMEM_LIMIT = V7X_VMEM_BYTES - 8 * 1024 * 1024


def _resident(shape):
    return pl.BlockSpec(shape, lambda *_: (0,) * len(shape), pipeline_mode=pl.Buffered(1))


def _rms(x, g):
    ms = jnp.mean(x * x, axis=-1, keepdims=True)
    return x * lax.rsqrt(ms + EPS) * g


def _swiglu(xn, wgu_ref, wdn_ref, acc_ref):
    for c in range(D_FF // FF_TILE):
        lo = c * FF_TILE
        w_g = wgu_ref[:, lo:lo + FF_TILE].astype(BF16)
        w_u = wgu_ref[:, D_FF + lo:D_FF + lo + FF_TILE].astype(BF16)
        g = jnp.dot(xn, w_g, preferred_element_type=F32)
        u = jnp.dot(xn, w_u, preferred_element_type=F32)
        a = (g * jax.nn.sigmoid(g) * u).astype(BF16)
        d = jnp.dot(a, wdn_ref[lo:lo + FF_TILE, :].astype(BF16), preferred_element_type=F32)
        if c == 0:
            acc_ref[...] = d
        else:
            acc_ref[...] += d


def _ffn1_kernel(x_ref, g_ref, wgu_ref, wdn_ref, *refs):
    n_next = (len(refs) - 2) // 2
    o_ref, acc_ref = refs[n_next], refs[-1]
    x = x_ref[...]
    _swiglu(_rms(x, g_ref[...]).astype(BF16), wgu_ref, wdn_ref, acc_ref)
    o_ref[...] = x + 0.5 * acc_ref[...]
    for src, dst in zip(refs[:n_next], refs[n_next + 1:-1]):
        dst[...] = src[...].astype(BF16)


def _ffn1(x, g, wgu, wdn, w_next):
    t = x.shape[0]
    steps = t // TOKEN_TILE
    row = pl.BlockSpec((TOKEN_TILE, D_MODEL), lambda i: (i, 0))
    assert all(w.shape[0] % (steps * CAST_ROWS) == 0 for w in w_next)
    cast_specs = [pl.BlockSpec((w.shape[0] // steps, w.shape[1]), lambda i: (i, 0))
                  for w in w_next]
    return pl.pallas_call(
        _ffn1_kernel,
        grid=(steps,),
        in_specs=[row, _resident((1, D_MODEL)), _resident(wgu.shape), _resident(wdn.shape)]
        + cast_specs,
        out_specs=[row] + cast_specs,
        out_shape=[jax.ShapeDtypeStruct(x.shape, F32)]
        + [jax.ShapeDtypeStruct(w.shape, BF16) for w in w_next],
        scratch_shapes=[pltpu.VMEM((TOKEN_TILE, D_MODEL), F32)],
        compiler_params=pltpu.CompilerParams(
            dimension_semantics=("arbitrary",), vmem_limit_bytes=VMEM_LIMIT),
        name="ffn1",
    )(x, g, wgu, wdn, *w_next)


def _head_norm(x, g2):
    lane = lax.broadcasted_iota(jnp.int32, x.shape, 1)
    lo = lane < HEAD_DIM
    sq = x * x
    s_lo = jnp.sum(jnp.where(lo, sq, 0.0), axis=-1, keepdims=True)
    s_hi = jnp.sum(jnp.where(lo, 0.0, sq), axis=-1, keepdims=True)
    ms = jnp.where(lo, s_lo, s_hi) * (1.0 / HEAD_DIM)
    return x * lax.rsqrt(ms + EPS) * g2


def _rotary(x, cos, sin_lo, sin_hi):
    half = ROT_DIM // 2
    up = pltpu.roll(x, LANES - half, 1)
    dn = pltpu.roll(x, half, 1)
    return x * cos + up * sin_lo + dn * sin_hi


def _in_proj_kernel(x_ref, g_ref, w_ref, qna_ref, kna_ref, qnb_ref, knb_ref,
                    cos_ref, sl_ref, sh_ref,
                    qa_ref, ka_ref, va_ref, qb_ref, kb_ref, vb_ref, gate_ref):
    h = _rms(x_ref[...], g_ref[...]).astype(BF16)
    scale = LOG2E / math.sqrt(HEAD_DIM)

    def proj(col, width):
        return jnp.dot(h, w_ref[:, col:col + width].astype(BF16), preferred_element_type=F32)

    def normed(col, gain_ref, out_ref, rotate, mult):
        y = proj(col, A_WIDTH)
        for p in range(A_WIDTH // LANES):
            t = _head_norm(y[:, p * LANES:(p + 1) * LANES], gain_ref[...])
            if rotate:
                t = _rotary(t, cos_ref[...], sl_ref[...], sh_ref[...])
            if mult != 1.0:
                t = t * mult
            out_ref[:, p * LANES:(p + 1) * LANES] = t.astype(BF16)

    normed(0 * A_WIDTH, qna_ref, qa_ref, False, scale)
    normed(1 * A_WIDTH, kna_ref, ka_ref, False, 1.0)
    va_ref[...] = proj(2 * A_WIDTH, A_WIDTH).astype(BF16)
    normed(3 * A_WIDTH, qnb_ref, qb_ref, True, scale)
    normed(4 * A_WIDTH, knb_ref, kb_ref, True, 1.0)
    vb_ref[...] = proj(5 * A_WIDTH, B_WIDTH).astype(BF16)
    for c in range(2 * D_MODEL // A_WIDTH):
        gate_ref[:, c * A_WIDTH:(c + 1) * A_WIDTH] = proj(6 * A_WIDTH + c * A_WIDTH, A_WIDTH)


def _in_proj(x1, g, w_in, qna, kna, qnb, knb, cos, sin_lo, sin_hi, seq):
    t = x1.shape[0]
    tiles_per_seq = seq // TOKEN_TILE
    row = lambda w: pl.BlockSpec((TOKEN_TILE, w), lambda i: (i, 0))
    pos = pl.BlockSpec((TOKEN_TILE, LANES), lambda i: (i % tiles_per_seq, 0))
    gain = _resident((1, LANES))
    half = jax.ShapeDtypeStruct((t, A_WIDTH), BF16)
    return pl.pallas_call(
        _in_proj_kernel,
        grid=(t // TOKEN_TILE,),
        in_specs=[row(D_MODEL), _resident((1, D_MODEL)), _resident(w_in.shape),
                  gain, gain, gain, gain, pos, pos, pos],
        out_specs=[row(A_WIDTH)] * 6 + [row(2 * D_MODEL)],
        out_shape=[half] * 6 + [jax.ShapeDtypeStruct((t, 2 * D_MODEL), F32)],
        compiler_params=pltpu.CompilerParams(
            dimension_semantics=("arbitrary",), vmem_limit_bytes=VMEM_LIMIT),
        name="in_proj",
    )(x1, g, w_in, qna, kna, qnb, knb, cos, sin_lo, sin_hi)


def _fill_band_bias(f_ref, bias_ref, shift):
    pad = LEFT_CHUNKS * CHUNK
    shape = (A_STRIP_ROWS, A_BIAS_W)
    q_chunk = lax.broadcasted_iota(jnp.int32, shape, 0) >> CHUNK_SHIFT
    k_chunk = (lax.broadcasted_iota(jnp.int32, shape, 1) - 2 * pad) >> CHUNK_SHIFT
    valid = jnp.logical_and(k_chunk <= q_chunk, k_chunk >= q_chunk - LEFT_CHUNKS)
    for h in range(A_HEADS):
        row = jnp.broadcast_to(f_ref[h:h + 1, :], (A_STRIP_ROWS, A_TABLE_W))
        skew = pltpu.roll(row, 0, 1, stride=1, stride_axis=0)
        bias_ref[h] = jnp.where(valid, (skew[:, :A_BIAS_W] - shift) * LOG2E, NEG)


def _attn_a_kernel(bound_ref, q_ref, k_ref, v_ref, f_ref, o_ref, bias_ref):
    i = pl.program_id(1)
    pad = LEFT_CHUNKS * CHUNK
    bound = bound_ref[0]
    shift_by_bound = bound <= SHIFT_SAFE

    @pl.when(jnp.logical_and(pl.program_id(0) == 0, i == 0))
    def _():
        _fill_band_bias(f_ref, bias_ref, jnp.where(shift_by_bound, bound, 0.0))

    lane = lax.broadcasted_iota(jnp.int32, (A_Q_TILE, LANES), 1)
    first = lane < HEAD_DIM

    def tile(shifted, sub):
        tile_start = (i * A_STEP_TILES + sub) * A_Q_TILE
        rows = slice(sub * A_Q_TILE, (sub + 1) * A_Q_TILE)
        start = pl.multiple_of(jnp.maximum(tile_start - pad, 0), A_Q_TILE)
        off = 2 * pad - tile_start + start
        for p in range(A_WIDTH // LANES):
            cols = slice(p * LANES, (p + 1) * LANES)
            q = q_ref[0, rows, cols]
            zero = jnp.zeros_like(q)
            qs = jnp.concatenate([jnp.where(first, q, zero), jnp.where(first, zero, q)], axis=0)
            k = k_ref[0, pl.ds(start, A_BAND), cols]
            v = v_ref[0, pl.ds(start, A_BAND), cols]
            s = lax.dot_general(qs, k, (((1,), (1,)), ((), ())), preferred_element_type=F32)
            bias = [bias_ref[2 * p + e, :,
                             pl.ds(pl.multiple_of(off - t * A_STRIP_ROWS, LANES), A_BAND)]
                    for e in range(2) for t in range(A_Q_TILE // A_STRIP_ROWS)]
            s = s + jnp.concatenate(bias, axis=0)
            if not shifted:
                s = s - jnp.max(s, axis=-1, keepdims=True)
            pr = jnp.exp2(s)
            l = jnp.sum(pr, axis=-1, keepdims=True)
            o = jnp.dot(pr.astype(BF16), v, preferred_element_type=F32) / l
            o_ref[0, rows, cols] = jnp.where(first, o[:A_Q_TILE], o[A_Q_TILE:]).astype(BF16)

    def step(shifted):
        for sub in range(A_STEP_TILES):
            tile(shifted, sub)

    pl.when(shift_by_bound)(functools.partial(step, True))
    pl.when(jnp.logical_not(shift_by_bound))(functools.partial(step, False))


def _attn_a(bound, q, k, v, f_table):
    b, s, _ = q.shape
    whole = pl.BlockSpec((1, s, A_WIDTH), lambda bi, i: (bi, 0, 0))
    step_rows = A_STEP_TILES * A_Q_TILE
    tile = pl.BlockSpec((1, step_rows, A_WIDTH), lambda bi, i: (bi, i, 0))
    return pl.pallas_call(
        _attn_a_kernel,
        grid=(b, s // step_rows),
        in_specs=[pl.BlockSpec(memory_space=pltpu.SMEM), tile, whole, whole,
                  _resident(f_table.shape)],
        out_specs=tile,
        out_shape=jax.ShapeDtypeStruct(q.shape, BF16),
        scratch_shapes=[pltpu.VMEM((A_HEADS, A_STRIP_ROWS, A_BIAS_W), F32)],
        compiler_params=pltpu.CompilerParams(
            dimension_semantics=("arbitrary", "arbitrary"), vmem_limit_bytes=VMEM_LIMIT),
        name="attn_a",
    )(bound, q, k, v, f_table)


def _clipped_table(rel_table):
    pad = LEFT_CHUNKS * CHUNK
    first, last = rel_table[:1], rel_table[-1:]
    lead = 2 * pad - REL_CLIP
    tail = A_STRIP_ROWS - 1
    mid = A_TABLE_W - lead - rel_table.shape[0] - tail
    f = jnp.concatenate([jnp.repeat(last, lead, axis=0), rel_table[::-1],
                         jnp.repeat(first, mid, axis=0), jnp.repeat(last, tail, axis=0)], axis=0)
    return f.T.astype(F32)


def _attn_b_kernel(bound_ref, q0_ref, q1_ref, k_ref, v_ref, lam_ref, gs_ref, o0_ref, o1_ref,
                   qs_ref, m_ref, l_ref, acc_ref):
    i = pl.program_id(1)
    n_tiles = k_ref.shape[1] // B_Q_TILE
    n_full = n_tiles - 1
    nn = (((1,), (1,)), ((), ()))
    lane = lax.broadcasted_iota(jnp.int32, (B_Q_TILE, LANES), 1)
    first = lane < HEAD_DIM

    def tile_rows(pair, slot):
        pos = pair if slot == 0 else B_STEP_PAIRS - 1 - pair
        return slice(pos * B_Q_TILE, (pos + 1) * B_Q_TILE)

    def stack_queries(pair):
        for slot, q_ref in enumerate((q0_ref, q1_ref)):
            for h in range(B_HEADS):
                q = q_ref[0, tile_rows(pair, slot), h * LANES:(h + 1) * LANES]
                zero = jnp.zeros_like(q)
                qs_ref[2 * pair + slot, h, :B_Q_TILE] = jnp.where(first, q, zero)
                qs_ref[2 * pair + slot, h, B_Q_TILE:] = jnp.where(first, zero, q)

    def scores(slot, h, lo, width, diag_col):
        k = k_ref[0, pl.ds(lo, width), h * LANES:(h + 1) * LANES]
        s = lax.dot_general(qs_ref[slot, h], k, nn, preferred_element_type=F32)
        if diag_col is not None:
            row = lax.broadcasted_iota(jnp.int32, s.shape, 0) & (B_Q_TILE - 1)
            col = lax.broadcasted_iota(jnp.int32, s.shape, 1) - diag_col
            s = jnp.where((col >> CHUNK_SHIFT) <= (row >> CHUNK_SHIFT), s, NEG)
        return [s[:, c * LANES:(c + 1) * LANES] for c in range(width // LANES)]

    all_heads = tuple(range(B_HEADS))

    def max_block(slot, lo, width, diag_col, assign, heads=all_heads):
        for h in heads:
            m = functools.reduce(jnp.maximum, scores(slot, h, lo, width, diag_col))
            m_ref[slot, h] = m if assign else jnp.maximum(m_ref[slot, h], m)

    def sum_block(shift, slot, lo, width, diag_col, assign, heads=all_heads):
        for h in heads:
            m = m_ref[slot, h] if shift is None else shift
            ps = [jnp.exp2(s - m) for s in scores(slot, h, lo, width, diag_col)]
            v = v_ref[0, pl.ds(lo, width), h * LANES:(h + 1) * LANES]
            l = functools.reduce(jnp.add, ps)
            pv = jnp.dot(jnp.concatenate(ps, axis=1).astype(BF16), v, preferred_element_type=F32)
            l_ref[slot, h] = l if assign else l_ref[slot, h] + l
            acc_ref[slot, h] = pv if assign else acc_ref[slot, h] + pv

    def sweep(block, unroll, pair, slot_done=None):
        a = i * B_STEP_PAIRS + pair
        for slot, tile in enumerate((a, n_tiles - 1 - a)):
            block(2 * pair + slot, pl.multiple_of(tile * B_Q_TILE, B_Q_TILE), B_Q_TILE, 0, True)

        def full(t):
            in_slot0 = t < a
            lo = jnp.where(in_slot0, t, t - a) * B_K_TILE
            block(2 * pair + jnp.where(in_slot0, 0, 1), pl.multiple_of(lo, B_K_TILE), B_K_TILE,
                  None, False)

        if unroll:
            shared = n_tiles // 2 - 1
            for t in range(shared):
                full(t)
            slot_done(0, all_heads)
            for h in all_heads:
                for t in range(shared, n_full):
                    block(2 * pair + 1, pl.multiple_of((t - a) * B_K_TILE, B_K_TILE), B_K_TILE,
                          None, False, heads=(h,))
                slot_done(1, (h,))
        else:
            def body(t, carry):
                full(t)
                return carry
            lax.fori_loop(0, n_full, body, 0)

    bound = bound_ref[0]
    shift_by_bound = bound <= SHIFT_SAFE

    def finish(pair, slot, heads=all_heads):
        lq1, lk1, lq2, lk2 = (lam_ref[n:n + 1, :] for n in range(4))
        lam = (jnp.exp(jnp.sum(lq1 * lk1, axis=-1, keepdims=True))
               - jnp.exp(jnp.sum(lq2 * lk2, axis=-1, keepdims=True)) + LAMBDA_INIT)
        o_ref = (o0_ref, o1_ref)[slot]
        idx = 2 * pair + slot
        for h in heads:
            o = acc_ref[idx, h] / jnp.sum(l_ref[idx, h], axis=-1, keepdims=True)
            o = o[:B_Q_TILE] - lam * o[B_Q_TILE:]
            o_ref[0, tile_rows(pair, slot), h * LANES:(h + 1) * LANES] = (
                _rms(o, gs_ref[...]) * (1.0 - LAMBDA_INIT)).astype(BF16)

    @pl.when(shift_by_bound)
    def _():
        for pair in range(B_STEP_PAIRS):
            stack_queries(pair)
            sweep(functools.partial(sum_block, bound * LOG2E), True, pair,
                  slot_done=functools.partial(finish, pair))

    @pl.when(jnp.logical_not(shift_by_bound))
    def _():
        for pair in range(B_STEP_PAIRS):
            stack_queries(pair)
            sweep(max_block, False, pair)
            for idx in (2 * pair, 2 * pair + 1):
                for h in range(B_HEADS):
                    m_ref[idx, h] = jnp.broadcast_to(
                        jnp.max(m_ref[idx, h], axis=-1, keepdims=True), m_ref.shape[2:])
            sweep(functools.partial(sum_block, None), False, pair)
            finish(pair, 0)
            finish(pair, 1)


def _attn_b(bound, q, k, v, lam_vecs, g_sub):
    b, s, _ = q.shape
    n_tiles = s // B_Q_TILE
    assert n_tiles % (2 * B_STEP_PAIRS) == 0 and B_K_TILE == B_Q_TILE
    steps = n_tiles // (2 * B_STEP_PAIRS)
    rows = B_STEP_PAIRS * B_Q_TILE
    whole = pl.BlockSpec((1, s, B_WIDTH), lambda bi, i: (bi, 0, 0))
    q_lo = pl.BlockSpec((1, rows, B_WIDTH), lambda bi, i: (bi, i, 0))
    q_hi = pl.BlockSpec((1, rows, B_WIDTH), lambda bi, i: (bi, 2 * steps - 1 - i, 0))
    o_hi = pl.BlockSpec((1, rows, B_WIDTH), lambda bi, i: (bi, steps - 1 - i, 0))
    stat = pltpu.VMEM((2 * B_STEP_PAIRS, B_HEADS, 2 * B_Q_TILE, LANES), F32)
    out = jax.ShapeDtypeStruct((b, s // 2, B_WIDTH), BF16)
    return pl.pallas_call(
        _attn_b_kernel,
        grid=(b, steps),
        in_specs=[pl.BlockSpec(memory_space=pltpu.SMEM), q_lo, q_hi, whole, whole,
                  _resident(lam_vecs.shape), _resident(g_sub.shape)],
        out_specs=[q_lo, o_hi],
        out_shape=[out, out],
        scratch_shapes=[pltpu.VMEM((2 * B_STEP_PAIRS, B_HEADS, 2 * B_Q_TILE, LANES), BF16),
                        stat, stat, stat],
        compiler_params=pltpu.CompilerParams(
            dimension_semantics=("arbitrary", "arbitrary"), vmem_limit_bytes=VMEM_LIMIT),
        name="attn_b",
    )(bound, q, q, k, v, lam_vecs, g_sub)


def _post_kernel(x_ref, oa_ref, ob_lo_ref, ob_hi_ref, gate_ref, wa_ref, wb_ref, wo_ref,
                 g2_ref, wgu_ref, wdn_ref, gf_ref, o_ref, acc_ref, *, tiles_per_seq):
    ya = jnp.dot(oa_ref[...], wa_ref[...].astype(BF16), preferred_element_type=F32)
    in_lo = (pl.program_id(0) % tiles_per_seq) < tiles_per_seq // 2
    ob = jnp.where(in_lo, ob_lo_ref[...], ob_hi_ref[...])
    yb = jnp.dot(ob, wb_ref[...].astype(BF16), preferred_element_type=F32)
    y = (jax.nn.sigmoid(gate_ref[:, :D_MODEL]) * ya
         + jax.nn.sigmoid(gate_ref[:, D_MODEL:]) * yb)
    x2 = x_ref[...] + jnp.dot(y.astype(BF16), wo_ref[...], preferred_element_type=F32)
    _swiglu(_rms(x2, g2_ref[...]).astype(BF16), wgu_ref, wdn_ref, acc_ref)
    o_ref[...] = _rms(x2 + 0.5 * acc_ref[...], gf_ref[...])


def _post(x1, oa, ob_lo, ob_hi, gates, wa, wb, wo, g2, wgu, wdn, gf, seq):
    t = x1.shape[0]
    tiles_per_seq = seq // TOKEN_TILE
    half = tiles_per_seq // 2
    assert tiles_per_seq % 2 == 0
    row = lambda w: pl.BlockSpec((TOKEN_TILE, w), lambda i: (i, 0))
    lo = pl.BlockSpec((TOKEN_TILE, B_WIDTH), lambda i: (
        (i // tiles_per_seq) * half + jnp.minimum(i % tiles_per_seq, half - 1), 0))
    hi = pl.BlockSpec((TOKEN_TILE, B_WIDTH), lambda i: (
        (i // tiles_per_seq) * half + jnp.maximum(i % tiles_per_seq - half, 0), 0))
    vec = _resident((1, D_MODEL))
    return pl.pallas_call(
        functools.partial(_post_kernel, tiles_per_seq=tiles_per_seq),
        grid=(t // TOKEN_TILE,),
        in_specs=[row(D_MODEL), row(A_WIDTH), lo, hi, row(2 * D_MODEL),
                  _resident(wa.shape), _resident(wb.shape), _resident(wo.shape),
                  vec, _resident(wgu.shape), _resident(wdn.shape), vec],
        out_specs=row(D_MODEL),
        out_shape=jax.ShapeDtypeStruct(x1.shape, F32),
        scratch_shapes=[pltpu.VMEM((TOKEN_TILE, D_MODEL), F32)],
        compiler_params=pltpu.CompilerParams(
            dimension_semantics=("arbitrary",), vmem_limit_bytes=VMEM_LIMIT),
        name="post",
    )(x1, oa, ob_lo, ob_hi, gates, wa, wb, wo, g2, wgu, wdn, gf)


def _rotary_tables(seq):
    half = ROT_DIM // 2
    pos = np.arange(seq, dtype=np.float64)
    inv = ROPE_THETA ** (-np.arange(0, ROT_DIM, 2, dtype=np.float64) / ROT_DIM)
    ang = pos[:, None] * inv[None, :]
    cos, sin = np.cos(ang), np.sin(ang)
    ones = np.ones((seq, HEAD_DIM - ROT_DIM))
    zeros = np.zeros((seq, HEAD_DIM - half))
    cos_h = np.concatenate([cos, cos, ones], axis=-1)
    sin_lo = np.concatenate([-sin, zeros], axis=-1)
    sin_hi = np.concatenate([np.zeros((seq, half)), sin,
                             np.zeros((seq, HEAD_DIM - ROT_DIM))], axis=-1)
    two = lambda a: jnp.asarray(np.concatenate([a, a], axis=-1), dtype=F32)
    return two(cos_h), two(sin_lo), two(sin_hi)


def kernel(x, g_ffn1, w_ffn1_gu, w_ffn1_down, g_mix, w_in, qn_a, kn_a, rel_bias, qn_b, kn_b,
           lambda_q1, lambda_k1, lambda_q2, lambda_k2, g_subln, w_up_a, w_up_b, w_out, g_ffn2,
           w_ffn2_gu, w_ffn2_down, g_final):
    b, s, d = x.shape
    assert d == D_MODEL and s % TOKEN_TILE == 0 and g_ffn1.shape[0] == 1
    l = 0
    t = b * s
    two = lambda g: jnp.concatenate([g, g], axis=-1)[None, :]
    cos, sin_lo, sin_hi = _rotary_tables(s)

    x1, wo, wgu2 = _ffn1(x.reshape(t, d), g_ffn1[l][None], w_ffn1_gu[l], w_ffn1_down[l],
                         [w_out[l], w_ffn2_gu[l]])
    qa, ka, va, qb, kb, vb, gates = _in_proj(
        x1, g_mix[l][None], w_in[l], two(qn_a[l]), two(kn_a[l]), two(qn_b[l]),
        two(kn_b[l]), cos, sin_lo, sin_hi, s)
    seq3 = lambda a: a.reshape(b, s, a.shape[-1])
    qk_bound = lambda gq, gk: (1.02 * math.sqrt(HEAD_DIM) * jnp.max(jnp.abs(gq))
                               * jnp.max(jnp.abs(gk))).reshape(1).astype(F32)
    bound_a = qk_bound(qn_a[l], kn_a[l]) + jnp.max(jnp.abs(rel_bias[l])).astype(F32)
    oa = _attn_a(bound_a, seq3(qa), seq3(ka), seq3(va), _clipped_table(rel_bias[l]))
    lam_vecs = jnp.stack([lambda_q1[l], lambda_k1[l], lambda_q2[l], lambda_k2[l]]).astype(F32)
    ob_lo, ob_hi = _attn_b(qk_bound(qn_b[l], kn_b[l]), seq3(qb), seq3(kb), seq3(vb), lam_vecs,
                           g_subln[l][None])
    out = _post(x1, oa.reshape(t, A_WIDTH), ob_lo.reshape(t // 2, B_WIDTH),
                ob_hi.reshape(t // 2, B_WIDTH), gates, w_up_a[l], w_up_b[l], wo,
                g_ffn2[l][None], wgu2, w_ffn2_down[l], g_final[l][None], s)
    return out.reshape(b, s, d)
```

```python
import functools
import math

import jax
import jax.numpy as jnp
import numpy as np
from jax import lax
from jax.experimental import pallas as pl
from jax.experimental.pallas import tpu as pltpu

F32 = jnp.float32
BF16 = jnp.bfloat16

D_MODEL = 1024
D_FF = 2816
CHUNK = 64
CHUNK_SHIFT = CHUNK.bit_length() - 1
assert 1 << CHUNK_SHIFT == CHUNK
LEFT_CHUNKS = 8
HEAD_DIM = 64
A_HEADS = 8
A_WIDTH = A_HEADS * HEAD_DIM
REL_CLIP = 256
B_HEADS = 4
B_WIDTH = B_HEADS * 2 * HEAD_DIM
ROPE_THETA = 500000.0
ROT_DIM = HEAD_DIM // 4
EPS = 1e-6
NEG = -1e30
LAMBDA_INIT = 0.8 - 0.6 * math.exp(-0.3 * 0)

LANES = 128
CAST_ROWS = 16
TOKEN_TILE = 512
FF_TILE = 256
A_Q_TILE = 256
A_BAND = A_Q_TILE + LEFT_CHUNKS * CHUNK
A_STEP_TILES = 4
A_STRIP_ROWS = 128
A_BIAS_W = 2 * LEFT_CHUNKS * CHUNK + A_BAND
A_TABLE_W = 2048
assert A_TABLE_W >= A_BIAS_W + A_STRIP_ROWS - 1 and A_Q_TILE % A_STRIP_ROWS == 0
B_Q_TILE = 256
B_K_TILE = B_Q_TILE
B_STEP_PAIRS = 1
SHIFT_SAFE = 40.0
V7X_VMEM_BYTES = 64 * 1024 * 1024
VMEM_LIMIT = V7X_VMEM_BYTES - 8 * 1024 * 1024


def _resident(shape):
    return pl.BlockSpec(shape, lambda *_: (0,) * len(shape), pipeline_mode=pl.Buffered(1))


def _rms(x, g):
    ms = jnp.mean(x * x, axis=-1, keepdims=True)
    return x * lax.rsqrt(ms + EPS) * g


def _swiglu(xn, wgu_ref, wdn_ref, acc_ref):
    for c in range(D_FF // FF_TILE):
        lo = c * FF_TILE
        w_g = wgu_ref[:, lo:lo + FF_TILE].astype(BF16)
        w_u = wgu_ref[:, D_FF + lo:D_FF + lo + FF_TILE].astype(BF16)
        g = jnp.dot(xn, w_g, preferred_element_type=F32)
        u = jnp.dot(xn, w_u, preferred_element_type=F32)
        a = (g * jax.nn.sigmoid(g) * u).astype(BF16)
        d = jnp.dot(a, wdn_ref[lo:lo + FF_TILE, :].astype(BF16), preferred_element_type=F32)
        if c == 0:
            acc_ref[...] = d
        else:
            acc_ref[...] += d


def _ffn1_kernel(x_ref, g_ref, wgu_ref, wdn_ref, *refs):
    n_next = (len(refs) - 2) // 2
    o_ref, acc_ref = refs[n_next], refs[-1]
    x = x_ref[...]
    _swiglu(_rms(x, g_ref[...]).astype(BF16), wgu_ref, wdn_ref, acc_ref)
    o_ref[...] = x + 0.5 * acc_ref[...]
    for src, dst in zip(refs[:n_next], refs[n_next + 1:-1]):
        dst[...] = src[...].astype(BF16)


def _ffn1(x, g, wgu, wdn, w_next):
    t = x.shape[0]
    steps = t // TOKEN_TILE
    row = pl.BlockSpec((TOKEN_TILE, D_MODEL), lambda i: (i, 0))
    assert all(w.shape[0] % (steps * CAST_ROWS) == 0 for w in w_next)
    cast_specs = [pl.BlockSpec((w.shape[0] // steps, w.shape[1]), lambda i: (i, 0))
                  for w in w_next]
    return pl.pallas_call(
        _ffn1_kernel,
        grid=(steps,),
        in_specs=[row, _resident((1, D_MODEL)), _resident(wgu.shape), _resident(wdn.shape)]
        + cast_specs,
        out_specs=[row] + cast_specs,
        out_shape=[jax.ShapeDtypeStruct(x.shape, F32)]
        + [jax.ShapeDtypeStruct(w.shape, BF16) for w in w_next],
        scratch_shapes=[pltpu.VMEM((TOKEN_TILE, D_MODEL), F32)],
        compiler_params=pltpu.CompilerParams(
            dimension_semantics=("parallel",), vmem_limit_bytes=VMEM_LIMIT),
        name="ffn1",
    )(x, g, wgu, wdn, *w_next)


def _head_norm(x, g2):
    lane = lax.broadcasted_iota(jnp.int32, x.shape, 1)
    lo = lane < HEAD_DIM
    sq = x * x
    s_lo = jnp.sum(jnp.where(lo, sq, 0.0), axis=-1, keepdims=True)
    s_hi = jnp.sum(jnp.where(lo, 0.0, sq), axis=-1, keepdims=True)
    ms = jnp.where(lo, s_lo, s_hi) * (1.0 / HEAD_DIM)
    return x * lax.rsqrt(ms + EPS) * g2


def _rotary(x, cos, sin_lo, sin_hi):
    half = ROT_DIM // 2
    up = pltpu.roll(x, LANES - half, 1)
    dn = pltpu.roll(x, half, 1)
    return x * cos + up * sin_lo + dn * sin_hi


def _in_proj_kernel(x_ref, g_ref, w_ref, qna_ref, kna_ref, qnb_ref, knb_ref,
                    cos_ref, sl_ref, sh_ref,
                    qa_ref, ka_ref, va_ref, qb_ref, kb_ref, vb_ref, gate_ref):
    h = _rms(x_ref[...], g_ref[...]).astype(BF16)
    scale = 1.0 / math.sqrt(HEAD_DIM)

    def proj(col, width):
        return jnp.dot(h, w_ref[:, col:col + width].astype(BF16), preferred_element_type=F32)

    def normed(col, gain_ref, out_ref, rotate, mult):
        y = proj(col, A_WIDTH)
        for p in range(A_WIDTH // LANES):
            t = _head_norm(y[:, p * LANES:(p + 1) * LANES], gain_ref[...])
            if rotate:
                t = _rotary(t, cos_ref[...], sl_ref[...], sh_ref[...])
            if mult != 1.0:
                t = t * mult
            out_ref[:, p * LANES:(p + 1) * LANES] = t.astype(BF16)

    normed(0 * A_WIDTH, qna_ref, qa_ref, False, scale)
    normed(1 * A_WIDTH, kna_ref, ka_ref, False, 1.0)
    va_ref[...] = proj(2 * A_WIDTH, A_WIDTH).astype(BF16)
    normed(3 * A_WIDTH, qnb_ref, qb_ref, True, scale)
    normed(4 * A_WIDTH, knb_ref, kb_ref, True, 1.0)
    vb_ref[...] = proj(5 * A_WIDTH, B_WIDTH).astype(BF16)
    for c in range(2 * D_MODEL // A_WIDTH):
        gate_ref[:, c * A_WIDTH:(c + 1) * A_WIDTH] = proj(6 * A_WIDTH + c * A_WIDTH, A_WIDTH)


def _in_proj(x1, g, w_in, qna, kna, qnb, knb, cos, sin_lo, sin_hi, seq):
    t = x1.shape[0]
    tiles_per_seq = seq // TOKEN_TILE
    row = lambda w: pl.BlockSpec((TOKEN_TILE, w), lambda i: (i, 0))
    pos = pl.BlockSpec((TOKEN_TILE, LANES), lambda i: (i % tiles_per_seq, 0))
    gain = _resident((1, LANES))
    half = jax.ShapeDtypeStruct((t, A_WIDTH), BF16)
    return pl.pallas_call(
        _in_proj_kernel,
        grid=(t // TOKEN_TILE,),
        in_specs=[row(D_MODEL), _resident((1, D_MODEL)), _resident(w_in.shape),
                  gain, gain, gain, gain, pos, pos, pos],
        out_specs=[row(A_WIDTH)] * 6 + [row(2 * D_MODEL)],
        out_shape=[half] * 6 + [jax.ShapeDtypeStruct((t, 2 * D_MODEL), F32)],
        compiler_params=pltpu.CompilerParams(
            dimension_semantics=("parallel",), vmem_limit_bytes=VMEM_LIMIT),
        name="in_proj",
    )(x1, g, w_in, qna, kna, qnb, knb, cos, sin_lo, sin_hi)


def _fill_band_bias(f_ref, bias_ref, shift):
    pad = LEFT_CHUNKS * CHUNK
    shape = (A_STRIP_ROWS, A_BIAS_W)
    q_chunk = lax.broadcasted_iota(jnp.int32, shape, 0) >> CHUNK_SHIFT
    k_chunk = (lax.broadcasted_iota(jnp.int32, shape, 1) - 2 * pad) >> CHUNK_SHIFT
    valid = jnp.logical_and(k_chunk <= q_chunk, k_chunk >= q_chunk - LEFT_CHUNKS)
    for h in range(A_HEADS):
        row = jnp.broadcast_to(f_ref[h:h + 1, :], (A_STRIP_ROWS, A_TABLE_W))
        skew = pltpu.roll(row, 0, 1, stride=1, stride_axis=0)
        bias_ref[h] = jnp.where(valid, skew[:, :A_BIAS_W] - shift, NEG)


def _attn_a_kernel(bound_ref, q_ref, k_ref, v_ref, f_ref, o_ref, bias_ref):
    i = pl.program_id(1)
    pad = LEFT_CHUNKS * CHUNK
    bound = bound_ref[0]
    shift_by_bound = bound <= SHIFT_SAFE

    @pl.when(jnp.logical_and(pl.program_id(0) == 0, i == 0))
    def _():
        _fill_band_bias(f_ref, bias_ref, jnp.where(shift_by_bound, bound, 0.0))

    lane = lax.broadcasted_iota(jnp.int32, (A_Q_TILE, LANES), 1)
    first = lane < HEAD_DIM

    def tile(shifted, sub):
        tile_start = (i * A_STEP_TILES + sub) * A_Q_TILE
        rows = slice(sub * A_Q_TILE, (sub + 1) * A_Q_TILE)
        start = pl.multiple_of(jnp.maximum(tile_start - pad, 0), A_Q_TILE)
        off = 2 * pad - tile_start + start
        for p in range(A_WIDTH // LANES):
            cols = slice(p * LANES, (p + 1) * LANES)
            q = q_ref[0, rows, cols]
            zero = jnp.zeros_like(q)
            qs = jnp.concatenate([jnp.where(first, q, zero), jnp.where(first, zero, q)], axis=0)
            k = k_ref[0, pl.ds(start, A_BAND), cols]
            v = v_ref[0, pl.ds(start, A_BAND), cols]
            s = lax.dot_general(qs, k, (((1,), (1,)), ((), ())), preferred_element_type=F32)
            bias = [bias_ref[2 * p + e, :,
                             pl.ds(pl.multiple_of(off - t * A_STRIP_ROWS, LANES), A_BAND)]
                    for e in range(2) for t in range(A_Q_TILE // A_STRIP_ROWS)]
            s = s + jnp.concatenate(bias, axis=0)
            if not shifted:
                s = s - jnp.max(s, axis=-1, keepdims=True)
            pr = jnp.exp(s)
            l = jnp.sum(pr, axis=-1, keepdims=True)
            o = jnp.dot(pr.astype(BF16), v, preferred_element_type=F32) / l
            o_ref[0, rows, cols] = jnp.where(first, o[:A_Q_TILE], o[A_Q_TILE:]).astype(BF16)

    def step(shifted):
        for sub in range(A_STEP_TILES):
            tile(shifted, sub)

    pl.when(shift_by_bound)(functools.partial(step, True))
    pl.when(jnp.logical_not(shift_by_bound))(functools.partial(step, False))


def _attn_a(bound, q, k, v, f_table):
    b, s, _ = q.shape
    whole = pl.BlockSpec((1, s, A_WIDTH), lambda bi, i: (bi, 0, 0))
    step_rows = A_STEP_TILES * A_Q_TILE
    tile = pl.BlockSpec((1, step_rows, A_WIDTH), lambda bi, i: (bi, i, 0))
    return pl.pallas_call(
        _attn_a_kernel,
        grid=(b, s // step_rows),
        in_specs=[pl.BlockSpec(memory_space=pltpu.SMEM), tile, whole, whole,
                  _resident(f_table.shape)],
        out_specs=tile,
        out_shape=jax.ShapeDtypeStruct(q.shape, BF16),
        scratch_shapes=[pltpu.VMEM((A_HEADS, A_STRIP_ROWS, A_BIAS_W), F32)],
        compiler_params=pltpu.CompilerParams(
            dimension_semantics=("arbitrary", "arbitrary"), vmem_limit_bytes=VMEM_LIMIT),
        name="attn_a",
    )(bound, q, k, v, f_table)


def _clipped_table(rel_table):
    pad = LEFT_CHUNKS * CHUNK
    first, last = rel_table[:1], rel_table[-1:]
    lead = 2 * pad - REL_CLIP
    tail = A_STRIP_ROWS - 1
    mid = A_TABLE_W - lead - rel_table.shape[0] - tail
    f = jnp.concatenate([jnp.repeat(last, lead, axis=0), rel_table[::-1],
                         jnp.repeat(first, mid, axis=0), jnp.repeat(last, tail, axis=0)], axis=0)
    return f.T.astype(F32)


def _attn_b_kernel(bound_ref, q0_ref, q1_ref, k_ref, v_ref, lam_ref, gs_ref, o0_ref, o1_ref,
                   qs_ref, m_ref, l_ref, acc_ref):
    i = pl.program_id(1)
    n_tiles = k_ref.shape[1] // B_Q_TILE
    n_full = n_tiles - 1
    nn = (((1,), (1,)), ((), ()))
    lane = lax.broadcasted_iota(jnp.int32, (B_Q_TILE, LANES), 1)
    first = lane < HEAD_DIM

    def tile_rows(pair, slot):
        pos = pair if slot == 0 else B_STEP_PAIRS - 1 - pair
        return slice(pos * B_Q_TILE, (pos + 1) * B_Q_TILE)

    def stack_queries(pair):
        for slot, q_ref in enumerate((q0_ref, q1_ref)):
            for h in range(B_HEADS):
                q = q_ref[0, tile_rows(pair, slot), h * LANES:(h + 1) * LANES]
                zero = jnp.zeros_like(q)
                qs_ref[2 * pair + slot, h, :B_Q_TILE] = jnp.where(first, q, zero)
                qs_ref[2 * pair + slot, h, B_Q_TILE:] = jnp.where(first, zero, q)

    def scores(slot, h, lo, width, diag_col):
        k = k_ref[0, pl.ds(lo, width), h * LANES:(h + 1) * LANES]
        s = lax.dot_general(qs_ref[slot, h], k, nn, preferred_element_type=F32)
        if diag_col is not None:
            row = lax.broadcasted_iota(jnp.int32, s.shape, 0) & (B_Q_TILE - 1)
            col = lax.broadcasted_iota(jnp.int32, s.shape, 1) - diag_col
            s = jnp.where((col >> CHUNK_SHIFT) <= (row >> CHUNK_SHIFT), s, NEG)
        return [s[:, c * LANES:(c + 1) * LANES] for c in range(width // LANES)]

    all_heads = tuple(range(B_HEADS))

    def max_block(slot, lo, width, diag_col, assign, heads=all_heads):
        for h in heads:
            m = functools.reduce(jnp.maximum, scores(slot, h, lo, width, diag_col))
            m_ref[slot, h] = m if assign else jnp.maximum(m_ref[slot, h], m)

    def sum_block(shift, slot, lo, width, diag_col, assign, heads=all_heads):
        for h in heads:
            m = m_ref[slot, h] if shift is None else shift
            ps = [jnp.exp(s - m) for s in scores(slot, h, lo, width, diag_col)]
            v = v_ref[0, pl.ds(lo, width), h * LANES:(h + 1) * LANES]
            l = functools.reduce(jnp.add, ps)
            pv = jnp.dot(jnp.concatenate(ps, axis=1).astype(BF16), v, preferred_element_type=F32)
            l_ref[slot, h] = l if assign else l_ref[slot, h] + l
            acc_ref[slot, h] = pv if assign else acc_ref[slot, h] + pv

    def sweep(block, unroll, pair, slot_done=None):
        a = i * B_STEP_PAIRS + pair
        for slot, tile in enumerate((a, n_tiles - 1 - a)):
            block(2 * pair + slot, pl.multiple_of(tile * B_Q_TILE, B_Q_TILE), B_Q_TILE, 0, True)

        def full(t):
            in_slot0 = t < a
            lo = jnp.where(in_slot0, t, t - a) * B_K_TILE
            block(2 * pair + jnp.where(in_slot0, 0, 1), pl.multiple_of(lo, B_K_TILE), B_K_TILE,
                  None, False)

        if unroll:
            shared = n_tiles // 2 - 1
            for t in range(shared):
                full(t)
            slot_done(0, all_heads)
            for h in all_heads:
                for t in range(shared, n_full):
                    block(2 * pair + 1, pl.multiple_of((t - a) * B_K_TILE, B_K_TILE), B_K_TILE,
                          None, False, heads=(h,))
                slot_done(1, (h,))
        else:
            def body(t, carry):
                full(t)
                return carry
            lax.fori_loop(0, n_full, body, 0)

    bound = bound_ref[0]
    shift_by_bound = bound <= SHIFT_SAFE

    def finish(pair, slot, heads=all_heads):
        lq1, lk1, lq2, lk2 = (lam_ref[n:n + 1, :] for n in range(4))
        lam = (jnp.exp(jnp.sum(lq1 * lk1, axis=-1, keepdims=True))
               - jnp.exp(jnp.sum(lq2 * lk2, axis=-1, keepdims=True)) + LAMBDA_INIT)
        o_ref = (o0_ref, o1_ref)[slot]
        idx = 2 * pair + slot
        for h in heads:
            o = acc_ref[idx, h] / jnp.sum(l_ref[idx, h], axis=-1, keepdims=True)
            o = o[:B_Q_TILE] - lam * o[B_Q_TILE:]
            o_ref[0, tile_rows(pair, slot), h * LANES:(h + 1) * LANES] = (
                _rms(o, gs_ref[...]) * (1.0 - LAMBDA_INIT)).astype(BF16)

    @pl.when(shift_by_bound)
    def _():
        for pair in range(B_STEP_PAIRS):
            stack_queries(pair)
            sweep(functools.partial(sum_block, bound), True, pair,
                  slot_done=functools.partial(finish, pair))

    @pl.when(jnp.logical_not(shift_by_bound))
    def _():
        for pair in range(B_STEP_PAIRS):
            stack_queries(pair)
            sweep(max_block, False, pair)
            for idx in (2 * pair, 2 * pair + 1):
                for h in range(B_HEADS):
                    m_ref[idx, h] = jnp.broadcast_to(
                        jnp.max(m_ref[idx, h], axis=-1, keepdims=True), m_ref.shape[2:])
            sweep(functools.partial(sum_block, None), False, pair)
            finish(pair, 0)
            finish(pair, 1)


def _attn_b(bound, q, k, v, lam_vecs, g_sub):
    b, s, _ = q.shape
    n_tiles = s // B_Q_TILE
    assert n_tiles % (2 * B_STEP_PAIRS) == 0 and B_K_TILE == B_Q_TILE
    steps = n_tiles // (2 * B_STEP_PAIRS)
    rows = B_STEP_PAIRS * B_Q_TILE
    whole = pl.BlockSpec((1, s, B_WIDTH), lambda bi, i: (bi, 0, 0))
    q_lo = pl.BlockSpec((1, rows, B_WIDTH), lambda bi, i: (bi, i, 0))
    q_hi = pl.BlockSpec((1, rows, B_WIDTH), lambda bi, i: (bi, 2 * steps - 1 - i, 0))
    o_hi = pl.BlockSpec((1, rows, B_WIDTH), lambda bi, i: (bi, steps - 1 - i, 0))
    stat = pltpu.VMEM((2 * B_STEP_PAIRS, B_HEADS, 2 * B_Q_TILE, LANES), F32)
    out = jax.ShapeDtypeStruct((b, s // 2, B_WIDTH), BF16)
    return pl.pallas_call(
        _attn_b_kernel,
        grid=(b, steps),
        in_specs=[pl.BlockSpec(memory_space=pltpu.SMEM), q_lo, q_hi, whole, whole,
                  _resident(lam_vecs.shape), _resident(g_sub.shape)],
        out_specs=[q_lo, o_hi],
        out_shape=[out, out],
        scratch_shapes=[pltpu.VMEM((2 * B_STEP_PAIRS, B_HEADS, 2 * B_Q_TILE, LANES), BF16),
                        stat, stat, stat],
        compiler_params=pltpu.CompilerParams(
            dimension_semantics=("arbitrary", "arbitrary"), vmem_limit_bytes=VMEM_LIMIT),
        name="attn_b",
    )(bound, q, q, k, v, lam_vecs, g_sub)


def _post_kernel(x_ref, oa_ref, ob_lo_ref, ob_hi_ref, gate_ref, wa_ref, wb_ref, wo_ref,
                 g2_ref, wgu_ref, wdn_ref, gf_ref, o_ref, acc_ref, *, tiles_per_seq):
    ya = jnp.dot(oa_ref[...], wa_ref[...].astype(BF16), preferred_element_type=F32)
    in_lo = (pl.program_id(0) % tiles_per_seq) < tiles_per_seq // 2
    ob = jnp.where(in_lo, ob_lo_ref[...], ob_hi_ref[...])
    yb = jnp.dot(ob, wb_ref[...].astype(BF16), preferred_element_type=F32)
    y = (jax.nn.sigmoid(gate_ref[:, :D_MODEL]) * ya
         + jax.nn.sigmoid(gate_ref[:, D_MODEL:]) * yb)
    x2 = x_ref[...] + jnp.dot(y.astype(BF16), wo_ref[...], preferred_element_type=F32)
    _swiglu(_rms(x2, g2_ref[...]).astype(BF16), wgu_ref, wdn_ref, acc_ref)
    o_ref[...] = _rms(x2 + 0.5 * acc_ref[...], gf_ref[...])


def _post(x1, oa, ob_lo, ob_hi, gates, wa, wb, wo, g2, wgu, wdn, gf, seq):
    t = x1.shape[0]
    tiles_per_seq = seq // TOKEN_TILE
    half = tiles_per_seq // 2
    assert tiles_per_seq % 2 == 0
    row = lambda w: pl.BlockSpec((TOKEN_TILE, w), lambda i: (i, 0))
    lo = pl.BlockSpec((TOKEN_TILE, B_WIDTH), lambda i: (
        (i // tiles_per_seq) * half + jnp.minimum(i % tiles_per_seq, half - 1), 0))
    hi = pl.BlockSpec((TOKEN_TILE, B_WIDTH), lambda i: (
        (i // tiles_per_seq) * half + jnp.maximum(i % tiles_per_seq - half, 0), 0))
    vec = _resident((1, D_MODEL))
    return pl.pallas_call(
        functools.partial(_post_kernel, tiles_per_seq=tiles_per_seq),
        grid=(t // TOKEN_TILE,),
        in_specs=[row(D_MODEL), row(A_WIDTH), lo, hi, row(2 * D_MODEL),
                  _resident(wa.shape), _resident(wb.shape), _resident(wo.shape),
                  vec, _resident(wgu.shape), _resident(wdn.shape), vec],
        out_specs=row(D_MODEL),
        out_shape=jax.ShapeDtypeStruct(x1.shape, F32),
        scratch_shapes=[pltpu.VMEM((TOKEN_TILE, D_MODEL), F32)],
        compiler_params=pltpu.CompilerParams(
            dimension_semantics=("parallel",), vmem_limit_bytes=VMEM_LIMIT),
        name="post",
    )(x1, oa, ob_lo, ob_hi, gates, wa, wb, wo, g2, wgu, wdn, gf)


def _rotary_tables(seq):
    half = ROT_DIM // 2
    pos = np.arange(seq, dtype=np.float64)
    inv = ROPE_THETA ** (-np.arange(0, ROT_DIM, 2, dtype=np.float64) / ROT_DIM)
    ang = pos[:, None] * inv[None, :]
    cos, sin = np.cos(ang), np.sin(ang)
    ones = np.ones((seq, HEAD_DIM - ROT_DIM))
    zeros = np.zeros((seq, HEAD_DIM - half))
    cos_h = np.concatenate([cos, cos, ones], axis=-1)
    sin_lo = np.concatenate([-sin, zeros], axis=-1)
    sin_hi = np.concatenate([np.zeros((seq, half)), sin,
                             np.zeros((seq, HEAD_DIM - ROT_DIM))], axis=-1)
    two = lambda a: jnp.asarray(np.concatenate([a, a], axis=-1), dtype=F32)
    return two(cos_h), two(sin_lo), two(sin_hi)


def kernel(x, g_ffn1, w_ffn1_gu, w_ffn1_down, g_mix, w_in, qn_a, kn_a, rel_bias, qn_b, kn_b,
           lambda_q1, lambda_k1, lambda_q2, lambda_k2, g_subln, w_up_a, w_up_b, w_out, g_ffn2,
           w_ffn2_gu, w_ffn2_down, g_final):
    b, s, d = x.shape
    assert d == D_MODEL and s % TOKEN_TILE == 0 and g_ffn1.shape[0] == 1
    l = 0
    t = b * s
    two = lambda g: jnp.concatenate([g, g], axis=-1)[None, :]
    cos, sin_lo, sin_hi = _rotary_tables(s)

    x1, wo, wgu2 = _ffn1(x.reshape(t, d), g_ffn1[l][None], w_ffn1_gu[l], w_ffn1_down[l],
                         [w_out[l], w_ffn2_gu[l]])
    qa, ka, va, qb, kb, vb, gates = _in_proj(
        x1, g_mix[l][None], w_in[l], two(qn_a[l]), two(kn_a[l]), two(qn_b[l]),
        two(kn_b[l]), cos, sin_lo, sin_hi, s)
    seq3 = lambda a: a.reshape(b, s, a.shape[-1])
    qk_bound = lambda gq, gk: (1.02 * math.sqrt(HEAD_DIM) * jnp.max(jnp.abs(gq))
                               * jnp.max(jnp.abs(gk))).reshape(1).astype(F32)
    bound_a = qk_bound(qn_a[l], kn_a[l]) + jnp.max(jnp.abs(rel_bias[l])).astype(F32)
    oa = _attn_a(bound_a, seq3(qa), seq3(ka), seq3(va), _clipped_table(rel_bias[l]))
    lam_vecs = jnp.stack([lambda_q1[l], lambda_k1[l], lambda_q2[l], lambda_k2[l]]).astype(F32)
    ob_lo, ob_hi = _attn_b(qk_bound(qn_b[l], kn_b[l]), seq3(qb), seq3(kb), seq3(vb), lam_vecs,
                           g_subln[l][None])
    out = _post(x1, oa.reshape(t, A_WIDTH), ob_lo.reshape(t // 2, B_WIDTH),
                ob_hi.reshape(t // 2, B_WIDTH), gates, w_up_a[l], w_up_b[l], wo,
                g_ffn2[l][None], wgu2, w_ffn2_down[l], g_final[l][None], s)
    return out.reshape(b, s, d)
```
